```python
import math
import jax, jax.numpy as jnp
from jax import lax
import numpy as np


D_MODEL = 2048
BATCH = 4
SEQ = 2048
DEPTH = 2
DEC_BATCH = 128
DEC_SEQ = 4
PAST_LEN = 16384
PAGE_SIZE = 128

D_MIX = D_MODEL
D_LRU = D_MIX // 2
D_SGU = D_MIX - D_LRU
LRU_HEADS = 8
LRU_HEAD_DIM = D_LRU // LRU_HEADS
CONV_WIDTH = 4
LRU_C = 8.0
CHUNK = 128
SGU_HEADS = 8
SGU_HEAD_DIM = D_SGU // SGU_HEADS
MEM_LEN = 256
XA_HEADS = 4
XA_HEAD_DIM = D_MODEL // XA_HEADS
D_FF = ((8 * D_MODEL // 3 + 127) // 128) * 128
HALF = 0.5
EPS = 1e-6

kernel_name = "hymba_rglru_sgu_macaron_step"


def rms_norm(x, g):
    xf = x.astype(jnp.float32)
    y = xf * lax.rsqrt(jnp.mean(xf * xf, axis=-1, keepdims=True) + EPS)
    return (y * g.astype(jnp.float32)).astype(x.dtype)


def layer_norm(x, g, b):
    xf = x.astype(jnp.float32)
    mu = jnp.mean(xf, axis=-1, keepdims=True)
    xc = xf - mu
    y = xc * lax.rsqrt(jnp.mean(xc * xc, axis=-1, keepdims=True) + EPS)
    return (y * g.astype(jnp.float32) + b.astype(jnp.float32)).astype(x.dtype)


def swiglu(x, w_in, w_down):
    gate, up = jnp.split(x @ w_in, 2, axis=-1)
    return (jax.nn.silu(gate) * up) @ w_down


def causal_conv(x, buf, w, b):
    xx = jnp.concatenate([buf.astype(x.dtype), x], axis=1)
    t = x.shape[1]
    y = b + sum(xx[:, k:k + t] * w[k] for k in range(CONV_WIDTH))
    return y, xx[:, -(CONV_WIDTH - 1):]


def _lru_combine(left, right):
    a_l, b_l = left
    a_r, b_r = right
    return a_l * a_r, a_r * b_l + b_r


def rg_lru(x, h0, w_a, b_a, w_i, b_i, lam):
    n, t, _ = x.shape
    xh = x.reshape(n, t, LRU_HEADS, LRU_HEAD_DIM)
    r = jax.nn.sigmoid(jnp.einsum('nthd,hde->nthe', xh, w_a) + b_a).reshape(n, t, D_LRU)
    i = jax.nn.sigmoid(jnp.einsum('nthd,hde->nthe', xh, w_i) + b_i).reshape(n, t, D_LRU)
    log_a = (-LRU_C * jax.nn.softplus(-lam.astype(jnp.float32))) * r.astype(jnp.float32)
    a = jnp.exp(log_a)
    b = jnp.sqrt(-jnp.expm1(2.0 * log_a)) * (i * x).astype(jnp.float32)
    b = b.at[:, 0].add(a[:, 0] * h0.astype(jnp.float32))
    _, h = lax.associative_scan(_lru_combine, (a, b), axis=1)
    return h, h[:, -1]


def spatial_gating(u, v, w_s, b_s):
    n, t, _ = u.shape
    rows = min(t, CHUNK)
    nc = t // rows
    mask = jnp.tril(jnp.ones((rows, rows), dtype=bool))
    ws = jnp.where(mask, w_s[:, :rows, :rows], 0).astype(v.dtype)
    vh = v.reshape(n, nc, rows, SGU_HEADS, SGU_HEAD_DIM)
    bias = b_s[:, :rows].T[:, :, None].astype(v.dtype)
    s = jnp.einsum('gts,ncsgd->nctgd', ws, vh) + bias
    return u * s.reshape(n, t, D_SGU)


def memory_kv(mem, g_mem, w_kv):
    n, m, _ = mem.shape
    k, v = jnp.split(rms_norm(mem, g_mem) @ w_kv, 2, axis=-1)
    return (k.reshape(n, m, XA_HEADS, XA_HEAD_DIM), v.reshape(n, m, XA_HEADS, XA_HEAD_DIM))


def cross_attention(h, k, v, w_q, w_o):
    n, t, _ = h.shape
    q = (h @ w_q).reshape(n, t, XA_HEADS, XA_HEAD_DIM)
    s = jnp.einsum('nthd,nmhd->nhtm', q, k.astype(q.dtype)).astype(jnp.float32) * (XA_HEAD_DIM ** -0.5)
    p = jax.nn.softmax(s, axis=-1).astype(q.dtype)
    o = jnp.einsum('nhtm,nmhd->nthd', p, v.astype(q.dtype)).reshape(n, t, D_MODEL)
    return o @ w_o


def decoder_layer(x, conv_buf, h0, mem_k, mem_v, p):
    h = rms_norm(x, p['ffn1_norm'][0])
    x = x + HALF * rms_norm(swiglu(h, p['ffn1_w_in'], p['ffn1_w_down']), p['ffn1_norm'][1])
    h = rms_norm(x, p['mix_norm'][0])
    z = h @ p['w_in']
    x_lru, g_lru, u_sgu, v_sgu = jnp.split(z, [D_LRU, 2 * D_LRU, 2 * D_LRU + D_SGU], axis=-1)
    xc, conv_new = causal_conv(x_lru, conv_buf, p['conv_w'], p['conv_b'])
    h_seq, h_last = rg_lru(xc, h0, p['lru_w_a'], p['lru_b_a'], p['lru_w_i'], p['lru_b_i'], p['lru_lambda'])
    y_lru = h_seq.astype(x.dtype) * jax.nn.gelu(g_lru)
    u = jax.nn.gelu(u_sgu)
    v = layer_norm(jax.nn.gelu(v_sgu), p['sgu_ln'][0], p['sgu_ln'][1])
    y_sgu = spatial_gating(u, v, p['sgu_w'], p['sgu_b'])
    mix = jnp.concatenate([y_lru, y_sgu], axis=-1) @ p['w_out']
    x = x + rms_norm(mix, p['mix_norm'][1])
    h = rms_norm(x, p['xa_norm'][0])
    x = x + rms_norm(cross_attention(h, mem_k, mem_v, p['xa_w_q'], p['xa_w_o']), p['xa_norm'][1])
    h = rms_norm(x, p['ffn2_norm'][0])
    x = x + HALF * rms_norm(swiglu(h, p['ffn2_w_in'], p['ffn2_w_down']), p['ffn2_norm'][1])
    return x, conv_new, h_last, v


def setup_inputs(seed: int = 0) -> dict:
    key = jax.random.key(seed)
    ks = iter(jax.random.split(key, 40))

    def nrm(shape, scale):
        return jax.random.normal(next(ks), shape, jnp.float32) * scale

    def gain(shape):
        return 1.0 + nrm(shape, 0.05)

    a8 = jax.random.uniform(next(ks), (DEPTH, D_LRU), jnp.float32, minval=0.9, maxval=0.999)
    a_base = a8 ** (1.0 / LRU_C)
    lru_lambda = jnp.log(a_base) - jnp.log1p(-a_base)
    return {
        'x_prompt': nrm((BATCH, SEQ, D_MODEL), 1.0),
        'x_sample': nrm((DEC_BATCH, DEC_SEQ, D_MODEL), 1.0),
        'mem_prompt': nrm((BATCH, MEM_LEN, D_MODEL), 1.0),
        'cache_mem_k': nrm((DEPTH, DEC_BATCH, MEM_LEN, XA_HEADS, XA_HEAD_DIM), 1.0),
        'cache_mem_v': nrm((DEPTH, DEC_BATCH, MEM_LEN, XA_HEADS, XA_HEAD_DIM), 1.0),
        'state_conv': nrm((DEPTH, DEC_BATCH, CONV_WIDTH - 1, D_LRU), 1.0),
        'state_lru_h': nrm((DEPTH, DEC_BATCH, D_LRU), 0.5),
        'ffn1_norm': gain((DEPTH, 2, D_MODEL)),
        'ffn1_w_in': nrm((DEPTH, D_MODEL, 2 * D_FF), D_MODEL ** -0.5),
        'ffn1_w_down': nrm((DEPTH, D_FF, D_MODEL), D_FF ** -0.5),
        'mix_norm': gain((DEPTH, 2, D_MODEL)),
        'w_in': nrm((DEPTH, D_MODEL, 2 * D_LRU + 2 * D_SGU), D_MODEL ** -0.5),
        'conv_w': nrm((DEPTH, CONV_WIDTH, D_LRU), CONV_WIDTH ** -0.5),
        'conv_b': nrm((DEPTH, D_LRU), 0.01),
        'lru_w_a': nrm((DEPTH, LRU_HEADS, LRU_HEAD_DIM, LRU_HEAD_DIM), LRU_HEAD_DIM ** -0.5),
        'lru_b_a': nrm((DEPTH, LRU_HEADS, LRU_HEAD_DIM), 0.01),
        'lru_w_i': nrm((DEPTH, LRU_HEADS, LRU_HEAD_DIM, LRU_HEAD_DIM), LRU_HEAD_DIM ** -0.5),
        'lru_b_i': nrm((DEPTH, LRU_HEADS, LRU_HEAD_DIM), 0.01),
        'lru_lambda': lru_lambda,
        'sgu_ln': jnp.stack([gain((DEPTH, D_SGU)), nrm((DEPTH, D_SGU), 0.01)], axis=1),
        'sgu_w': nrm((DEPTH, SGU_HEADS, CHUNK, CHUNK), CHUNK ** -0.5),
        'sgu_b': gain((DEPTH, SGU_HEADS, CHUNK)),
        'w_out': nrm((DEPTH, D_MIX, D_MODEL), D_MIX ** -0.5),
        'xa_norm': gain((DEPTH, 3, D_MODEL)),
        'xa_w_q': nrm((DEPTH, D_MODEL, D_MODEL), D_MODEL ** -0.5),
        'xa_w_kv': nrm((DEPTH, D_MODEL, 2 * D_MODEL), D_MODEL ** -0.5),
        'xa_w_o': nrm((DEPTH, D_MODEL, D_MODEL), D_MODEL ** -0.5),
        'ffn2_norm': gain((DEPTH, 2, D_MODEL)),
        'ffn2_w_in': nrm((DEPTH, D_MODEL, 2 * D_FF), D_MODEL ** -0.5),
        'ffn2_w_down': nrm((DEPTH, D_FF, D_MODEL), D_FF ** -0.5),
    }


def reference(x_prompt, x_sample, mem_prompt, cache_mem_k, cache_mem_v, state_conv, state_lru_h,
              ffn1_norm, ffn1_w_in, ffn1_w_down, mix_norm, w_in, conv_w, conv_b,
              lru_w_a, lru_b_a, lru_w_i, lru_b_i, lru_lambda, sgu_ln, sgu_w, sgu_b, w_out,
              xa_norm, xa_w_q, xa_w_kv, xa_w_o, ffn2_norm, ffn2_w_in, ffn2_w_down):
    yp, ys = x_prompt, x_sample
    n_p = x_prompt.shape[0]
    mk_p, mv_p, cv_p, h_p, cv_s, h_s, v_s = [], [], [], [], [], [], []
    for l in range(DEPTH):
        p = {
            'ffn1_norm': ffn1_norm[l], 'ffn1_w_in': ffn1_w_in[l], 'ffn1_w_down': ffn1_w_down[l],
            'mix_norm': mix_norm[l], 'w_in': w_in[l], 'conv_w': conv_w[l], 'conv_b': conv_b[l],
            'lru_w_a': lru_w_a[l], 'lru_b_a': lru_b_a[l], 'lru_w_i': lru_w_i[l], 'lru_b_i': lru_b_i[l],
            'lru_lambda': lru_lambda[l], 'sgu_ln': sgu_ln[l], 'sgu_w': sgu_w[l], 'sgu_b': sgu_b[l],
            'w_out': w_out[l], 'xa_norm': xa_norm[l], 'xa_w_q': xa_w_q[l], 'xa_w_o': xa_w_o[l],
            'ffn2_norm': ffn2_norm[l], 'ffn2_w_in': ffn2_w_in[l], 'ffn2_w_down': ffn2_w_down[l],
        }
        k_mem, v_mem = memory_kv(mem_prompt, xa_norm[l, 2], xa_w_kv[l])
        yp, conv_new, h_last, _ = decoder_layer(
            yp, jnp.zeros((n_p, CONV_WIDTH - 1, D_LRU), yp.dtype),
            jnp.zeros((n_p, D_LRU), jnp.float32), k_mem, v_mem, p)
        mk_p.append(k_mem)
        mv_p.append(v_mem)
        cv_p.append(conv_new)
        h_p.append(h_last)
        ys, conv_new_s, h_last_s, v_rows = decoder_layer(
            ys, state_conv[l], state_lru_h[l], cache_mem_k[l], cache_mem_v[l], p)
        cv_s.append(conv_new_s)
        h_s.append(h_last_s)
        v_s.append(v_rows)
    return (yp, ys, jnp.stack(mk_p), jnp.stack(mv_p), jnp.stack(cv_p), jnp.stack(h_p),
            jnp.stack(cv_s), jnp.stack(h_s), jnp.stack(v_s))
```

```python
import functools
import math

import jax
import jax.numpy as jnp
from jax import lax
from jax.experimental import pallas as pl
from jax.experimental.pallas import tpu as pltpu

F32 = jnp.float32
BF16 = jnp.bfloat16

EPS = 1e-6
HALF = 0.5
LRU_C = 8.0
CONV_WIDTH = 4
CHUNK = 128
LRU_HEADS = 8
SGU_HEADS = 8
XA_HEADS = 4

VMEM_LIMIT_BYTES = 56 * 1024 * 1024
SUBLANES = 8
LANES = 128
FF_TILE = 512
SAMPLE_GROUP = 8
LRU_PAIR = 2 * LANES


def _params(*sem):
    return pltpu.CompilerParams(dimension_semantics=sem, vmem_limit_bytes=VMEM_LIMIT_BYTES)


def _pick_tile(n, prefs):
    for p in prefs:
        if n % p == 0:
            return p
    raise ValueError(f"no tile in {prefs} divides {n}")


def _rms(x, g):
    return (x * lax.rsqrt(jnp.mean(x * x, axis=-1, keepdims=True) + EPS)) * g


def _gelu(x):
    c = math.sqrt(2.0 / math.pi)
    return 0.5 * x * (1.0 + jnp.tanh(c * (x + 0.044715 * (x * x * x))))


def _softplus(x):
    return jnp.maximum(x, 0.0) + jnp.log1p(jnp.exp(-jnp.abs(x)))


def _ffn_kernel(x_ref, g_ref, wg_ref, wu_ref, wd_ref, o_ref, h_scr, *, nf):
    f = pl.program_id(1)

    @pl.when(f == 0)
    def _():
        h_scr[...] = _rms(x_ref[...], g_ref[0:1, :]).astype(BF16)

    h = h_scr[...]
    gate = jnp.dot(h, wg_ref[...], preferred_element_type=F32)
    up = jnp.dot(h, wu_ref[...], preferred_element_type=F32)
    act = ((gate * jax.nn.sigmoid(gate)) * up).astype(BF16)
    part = jnp.dot(act, wd_ref[...], preferred_element_type=F32)

    @pl.when(f == 0)
    def _():
        o_ref[...] = part

    @pl.when(f > 0)
    def _():
        o_ref[...] += part

    @pl.when(f == nf - 1)
    def _():
        o_ref[...] = x_ref[...] + HALF * _rms(o_ref[...], g_ref[1:2, :])


def _ffn(x, norm, wg, wu, wd, tm):
    m, d = x.shape
    ff = wg.shape[1]
    nf = ff // FF_TILE
    return pl.pallas_call(
        functools.partial(_ffn_kernel, nf=nf),
        grid=(m // tm, nf),
        in_specs=[
            pl.BlockSpec((tm, d), lambda i, f: (i, 0)),
            pl.BlockSpec((2, d), lambda i, f: (0, 0)),
            pl.BlockSpec((d, FF_TILE), lambda i, f: (0, f)),
            pl.BlockSpec((d, FF_TILE), lambda i, f: (0, f)),
            pl.BlockSpec((FF_TILE, d), lambda i, f: (f, 0)),
        ],
        out_specs=pl.BlockSpec((tm, d), lambda i, f: (i, 0)),
        out_shape=jax.ShapeDtypeStruct((m, d), F32),
        scratch_shapes=[pltpu.VMEM((tm, d), BF16)],
        compiler_params=_params("parallel", "arbitrary"),
        name="ffn",
    )(x, norm, wg, wu, wd)


def _norm_matmul_kernel(x_ref, g_ref, w_ref, o_ref, h_scr):
    @pl.when(pl.program_id(1) == 0)
    def _():
        h_scr[...] = _rms(x_ref[...], g_ref[...]).astype(BF16)

    o_ref[...] = jnp.dot(h_scr[...], w_ref[...], preferred_element_type=F32).astype(o_ref.dtype)


def _norm_matmul(x, g, w, tm, tn, out_dtype, name):
    m, d = x.shape
    n = w.shape[1]
    return pl.pallas_call(
        _norm_matmul_kernel,
        grid=(m // tm, n // tn),
        in_specs=[
            pl.BlockSpec((tm, d), lambda i, j: (i, 0)),
            pl.BlockSpec((1, d), lambda i, j: (0, 0)),
            pl.BlockSpec((d, tn), lambda i, j: (0, j)),
        ],
        out_specs=pl.BlockSpec((tm, tn), lambda i, j: (i, j)),
        out_shape=jax.ShapeDtypeStruct((m, n), out_dtype),
        scratch_shapes=[pltpu.VMEM((tm, d), BF16)],
        compiler_params=_params("parallel", "arbitrary"),
        name=name,
    )(x, g, w)


def _matmul_norm_res_kernel(y_ref, w_ref, x_ref, g_ref, o_ref):
    d = jnp.dot(y_ref[...], w_ref[...], preferred_element_type=F32)
    o_ref[...] = x_ref[...] + _rms(d, g_ref[...])


def _matmul_norm_res(y, w, x, g, tm, name):
    m, k = y.shape
    n = w.shape[1]
    return pl.pallas_call(
        _matmul_norm_res_kernel,
        grid=(m // tm,),
        in_specs=[
            pl.BlockSpec((tm, k), lambda i: (i, 0)),
            pl.BlockSpec((k, n), lambda i: (0, 0)),
            pl.BlockSpec((tm, n), lambda i: (i, 0)),
            pl.BlockSpec((1, n), lambda i: (0, 0)),
        ],
        out_specs=pl.BlockSpec((tm, n), lambda i: (i, 0)),
        out_shape=jax.ShapeDtypeStruct((m, n), F32),
        compiler_params=_params("parallel"),
        name=name,
    )(y, w, x, g)


def _lru_gates(xc, wg, bg, c):
    g = jnp.dot(xc.astype(BF16), wg, preferred_element_type=F32) + bg
    r = jax.nn.sigmoid(g[:, :LRU_PAIR])
    i = jax.nn.sigmoid(g[:, LRU_PAIR:])
    log_a = c * r
    a = jnp.exp(log_a)
    b = jnp.sqrt(-jnp.tanh(log_a) * (a * a + 1.0)) * (i * xc)
    return a, b


def _layer_norm(x, g, b):
    mu = jnp.mean(x, axis=-1, keepdims=True)
    xc = x - mu
    return (xc * lax.rsqrt(jnp.mean(xc * xc, axis=-1, keepdims=True) + EPS)) * g + b


def _mixer_prompt_kernel(z_ref, cw_ref, cb_ref, wg_ref, bg_ref, lam_ref, ln_ref, sw_ref, sb_ref,
                         y_ref, xlast_ref, hlast_ref,
                         xprev_scr, hc_scr, xc_scr, a_scr, b_scr, h_scr, *, tt, dl, ds):
    t = pl.program_id(1)
    ng = tt // SUBLANES

    @pl.when(t == 0)
    def _():
        xprev_scr[...] = jnp.zeros_like(xprev_scr)
        hc_scr[...] = jnp.zeros_like(hc_scr)

    x = z_ref[:, 0:dl]
    w = cw_ref[...]
    cb = cb_ref[...]
    acc = cb + w[CONV_WIDTH - 1:CONV_WIDTH, :] * x
    for k in range(1, CONV_WIDTH):
        acc = acc + w[CONV_WIDTH - 1 - k:CONV_WIDTH - k, :] * pltpu.roll(x, k, 0)
    xc_scr[...] = acc
    x_head = x[0:SUBLANES, :]
    ext = jnp.concatenate([xprev_scr[...], x_head], axis=0)
    acc = cb + w[CONV_WIDTH - 1:CONV_WIDTH, :] * x_head
    for k in range(1, CONV_WIDTH):
        acc = acc + w[CONV_WIDTH - 1 - k:CONV_WIDTH - k, :] * pltpu.roll(ext, k, 0)[SUBLANES:, :]
    xc_scr[0:SUBLANES, :] = acc
    x_tail = x[tt - SUBLANES:tt, :]
    xprev_scr[...] = x_tail
    xlast_ref[0] = x_tail

    c_all = -LRU_C * _softplus(-lam_ref[...])
    sidx = lax.broadcasted_iota(jnp.int32, (ng, SUBLANES, LRU_PAIR), 1)
    for p in range(dl // LRU_PAIR):
        sl = slice(p * LRU_PAIR, (p + 1) * LRU_PAIR)
        a, b = _lru_gates(xc_scr[:, sl], wg_ref[p], bg_ref[p], c_all[:, sl])
        a = a.reshape(ng, SUBLANES, LRU_PAIR)
        b = b.reshape(ng, SUBLANES, LRU_PAIR)
        shift = 1
        while shift < SUBLANES:
            keep = sidx >= shift
            b = jnp.where(keep, a * pltpu.roll(b, shift, 1) + b, b)
            a = jnp.where(keep, a * pltpu.roll(a, shift, 1), a)
            shift *= 2
        a_scr[:, :, sl] = a
        b_scr[:, :, sl] = b

    def group_step(gi, carry):
        h = a_scr[gi] * carry + b_scr[gi]
        h_scr[gi] = h
        return h[SUBLANES - 1:SUBLANES, :]

    carry = lax.fori_loop(0, ng, group_step, hc_scr[...])
    hc_scr[...] = carry
    hlast_ref[0] = carry
    h_seq = h_scr[...].reshape(tt, dl)
    y_ref[:, 0:dl] = (h_seq * _gelu(z_ref[:, dl:2 * dl])).astype(BF16)

    u = _gelu(z_ref[:, 2 * dl:2 * dl + ds])
    v = _layer_norm(_gelu(z_ref[:, 2 * dl + ds:2 * dl + 2 * ds]), ln_ref[0:1, :], ln_ref[1:2, :])
    vb = v.astype(BF16)
    nc = tt // CHUNK
    hd = ds // SGU_HEADS
    row = lax.broadcasted_iota(jnp.int32, (CHUNK, CHUNK), 0)
    col = lax.broadcasted_iota(jnp.int32, (CHUNK, CHUNK), 1)
    for g in range(SGU_HEADS):
        cs = slice(g * hd, (g + 1) * hd)
        wt = jnp.where(row >= col, sw_ref[g], 0.0).astype(BF16)
        vg = jnp.concatenate([vb[c * CHUNK:(c + 1) * CHUNK, cs] for c in range(nc)], axis=1)
        s = jnp.dot(wt, vg, preferred_element_type=F32)
        for c in range(nc):
            rs = slice(c * CHUNK, (c + 1) * CHUNK)
            sc = s[:, c * hd:(c + 1) * hd] + sb_ref[:, cs]
            y_ref[rs, dl + g * hd:dl + (g + 1) * hd] = (u[rs, cs] * sc).astype(BF16)


def _mixer_prompt(z, nbatch, seq, cw, cb, wg, bg, lam, ln, sw, sb, tt):
    dz = z.shape[1]
    dl = cw.shape[1]
    ds = ln.shape[1]
    nt = seq // tt
    ng = tt // SUBLANES
    full = lambda *shape: pl.BlockSpec(shape, lambda b, t: (0,) * len(shape))
    return pl.pallas_call(
        functools.partial(_mixer_prompt_kernel, tt=tt, dl=dl, ds=ds),
        grid=(nbatch, nt),
        in_specs=[
            pl.BlockSpec((tt, dz), lambda b, t: (b * nt + t, 0)),
            full(*cw.shape), full(*cb.shape), full(*wg.shape), full(*bg.shape), full(*lam.shape),
            full(*ln.shape), full(*sw.shape), full(*sb.shape),
        ],
        out_specs=[
            pl.BlockSpec((tt, dl + ds), lambda b, t: (b * nt + t, 0)),
            pl.BlockSpec((1, SUBLANES, dl), lambda b, t: (b, 0, 0)),
            pl.BlockSpec((1, 1, dl), lambda b, t: (b, 0, 0)),
        ],
        out_shape=[
            jax.ShapeDtypeStruct((nbatch * seq, dl + ds), BF16),
            jax.ShapeDtypeStruct((nbatch, SUBLANES, dl), F32),
            jax.ShapeDtypeStruct((nbatch, 1, dl), F32),
        ],
        scratch_shapes=[
            pltpu.VMEM((SUBLANES, dl), F32),
            pltpu.VMEM((1, dl), F32),
            pltpu.VMEM((tt, dl), F32),
            pltpu.VMEM((ng, SUBLANES, dl), F32),
            pltpu.VMEM((ng, SUBLANES, dl), F32),
            pltpu.VMEM((ng, SUBLANES, dl), F32),
        ],
        compiler_params=_params("arbitrary", "arbitrary"),
        name="mixer_prompt",
    )(z, cw, cb, wg, bg, lam, ln, sw, sb)


def _mixer_sample_kernel(z_ref, cbuf_ref, h0_ref, cw_ref, cb_ref, wg_ref, bg_ref, lam_ref, ln_ref,
                         swx_ref, sbx_ref,
                         y_ref, convnew_ref, hlast_ref, v_ref, xc_scr, *, nb, ts, dl, ds):
    rows = lambda t: slice(t * nb, (t + 1) * nb)
    hist = CONV_WIDTH - 1
    xx = [cbuf_ref[k] for k in range(hist)] + [z_ref[rows(t), 0:dl] for t in range(ts)]
    w = cw_ref[...]
    cb = cb_ref[...]
    for t in range(ts):
        acc = cb + w[0:1, :] * xx[t]
        for k in range(1, CONV_WIDTH):
            acc = acc + w[k:k + 1, :] * xx[t + k]
        xc_scr[rows(t), :] = acc
    for k in range(hist):
        convnew_ref[k] = xx[ts + k]

    c_all = -LRU_C * _softplus(-lam_ref[...])
    for p in range(dl // LRU_PAIR):
        sl = slice(p * LRU_PAIR, (p + 1) * LRU_PAIR)
        a, b = _lru_gates(xc_scr[:, sl], wg_ref[p], bg_ref[p], c_all[:, sl])
        h = h0_ref[:, sl]
        for t in range(ts):
            h = a[rows(t), :] * h + b[rows(t), :]
            gl = z_ref[rows(t), dl + p * LRU_PAIR:dl + (p + 1) * LRU_PAIR]
            y_ref[rows(t), sl] = (h * _gelu(gl)).astype(BF16)
        hlast_ref[:, sl] = h

    vs = []
    for t in range(ts):
        v = _layer_norm(_gelu(z_ref[rows(t), 2 * dl + ds:2 * dl + 2 * ds]), ln_ref[0:1, :], ln_ref[1:2, :])
        v_ref[t] = v
        vs.append(v)
    for t in range(ts):
        s = sbx_ref[t:t + 1, :] + swx_ref[t * ts:t * ts + 1, :] * vs[0]
        for j in range(1, t + 1):
            s = s + swx_ref[t * ts + j:t * ts + j + 1, :] * vs[j]
        u = _gelu(z_ref[rows(t), 2 * dl:2 * dl + ds])
        y_ref[rows(t), dl:dl + ds] = (u * s).astype(BF16)


def _mixer_sample(z, cbuf, h0, cw, cb, wg, bg, lam, ln, swx, sbx, nb, ts):
    dl = cw.shape[1]
    ds = ln.shape[1]
    return pl.pallas_call(
        functools.partial(_mixer_sample_kernel, nb=nb, ts=ts, dl=dl, ds=ds),
        out_shape=[
            jax.ShapeDtypeStruct((ts * nb, dl + ds), BF16),
            jax.ShapeDtypeStruct((CONV_WIDTH - 1, nb, dl), F32),
            jax.ShapeDtypeStruct((nb, dl), F32),
            jax.ShapeDtypeStruct((ts, nb, ds), F32),
        ],
        scratch_shapes=[pltpu.VMEM((ts * nb, dl), F32)],
        compiler_params=pltpu.CompilerParams(vmem_limit_bytes=VMEM_LIMIT_BYTES),
        name="mixer_sample",
    )(z, cbuf, h0, cw, cb, wg, bg, lam, ln, swx, sbx)


def _softmax_rows(s):
    e = jnp.exp(s - jnp.max(s, axis=-1, keepdims=True))
    return e / jnp.sum(e, axis=-1, keepdims=True)


def _attn_prompt_kernel(q_ref, kt_ref, v_ref, o_ref, *, hd, scale):
    for h in range(XA_HEADS):
        hs = slice(h * hd, (h + 1) * hd)
        s = jnp.dot(q_ref[:, hs], kt_ref[0, hs, :], preferred_element_type=F32) * scale
        p = _softmax_rows(s).astype(BF16)
        o_ref[:, hs] = jnp.dot(p, v_ref[0, :, hs], preferred_element_type=F32).astype(BF16)


def _attn_prompt(q, kt, v, nbatch, seq, tq):
    d = q.shape[1]
    mem = v.shape[1]
    hd = d // XA_HEADS
    nt = seq // tq
    return pl.pallas_call(
        functools.partial(_attn_prompt_kernel, hd=hd, scale=hd ** -0.5),
        grid=(nbatch, nt),
        in_specs=[
            pl.BlockSpec((tq, d), lambda b, t: (b * nt + t, 0)),
            pl.BlockSpec((1, d, mem), lambda b, t: (b, 0, 0)),
            pl.BlockSpec((1, mem, d), lambda b, t: (b, 0, 0)),
        ],
        out_specs=pl.BlockSpec((tq, d), lambda b, t: (b * nt + t, 0)),
        out_shape=jax.ShapeDtypeStruct((nbatch * seq, d), BF16),
        compiler_params=_params("parallel", "parallel"),
        name="attn_prompt",
    )(q, kt, v)


def _attn_sample_kernel(qt_ref, k_ref, v_ref, o_ref, *, ts, scale):
    gb, mem, hd = k_ref.shape
    kb = k_ref[...].reshape(gb * mem, hd).astype(BF16)
    st = jnp.dot(kb, qt_ref[0, 0], preferred_element_type=F32) * scale
    st = st.reshape(gb, mem, LANES)
    e = jnp.exp(st - jnp.max(st, axis=1, keepdims=True))
    p = e / jnp.sum(e, axis=1, keepdims=True)
    own = (lax.broadcasted_iota(jnp.int32, (gb, mem, LANES), 2) // ts
           == lax.broadcasted_iota(jnp.int32, (gb, mem, LANES), 0))
    pm = jnp.where(own, p, 0.0).astype(BF16).reshape(gb * mem, LANES)
    vb = v_ref[...].reshape(gb * mem, hd).astype(BF16)
    o = lax.dot_general(pm, vb, (((0,), (0,)), ((), ())), preferred_element_type=F32)
    o_ref[0] = o[0:gb * ts, :].astype(BF16)


def _attn_sample(qt, k_all, v_all, layer, ts):
    ngroups = qt.shape[0]
    _, nb, mem, d = k_all.shape
    hd = d // XA_HEADS
    gb = nb // ngroups
    kv_spec = pl.BlockSpec((None, gb, mem, hd), lambda i, h: (layer, i, 0, h))
    return pl.pallas_call(
        functools.partial(_attn_sample_kernel, ts=ts, scale=hd ** -0.5),
        grid=(ngroups, XA_HEADS),
        in_specs=[
            pl.BlockSpec((1, 1, hd, LANES), lambda i, h: (i, h, 0, 0)),
            kv_spec,
            kv_spec,
        ],
        out_specs=pl.BlockSpec((1, gb * ts, hd), lambda i, h: (i, 0, h)),
        out_shape=jax.ShapeDtypeStruct((ngroups, gb * ts, d), BF16),
        compiler_params=_params("parallel", "parallel"),
        name="attn_sample",
    )(qt, k_all, v_all)


def _pad_to(a, axis, size):
    pad = [(0, 0)] * a.ndim
    pad[axis] = (0, size - a.shape[axis])
    return jnp.pad(a, pad)


def _gate_weights(w_a, b_a, w_i, b_i):
    heads, hd, _ = w_a.shape
    per = LRU_PAIR // hd
    npair = heads // per

    def blockdiag(w):
        w = w.reshape(npair, per, hd, hd)
        eye = jnp.eye(per, dtype=w.dtype)
        return jnp.einsum('pade,ab->padbe', w, eye).reshape(npair, per * hd, per * hd)

    wg = jnp.concatenate([blockdiag(w_a), blockdiag(w_i)], axis=-1).astype(BF16)
    bg = jnp.concatenate([b_a.reshape(npair, 1, per * hd), b_i.reshape(npair, 1, per * hd)], axis=-1)
    return wg, bg


def kernel(x_prompt, x_sample, mem_prompt, cache_mem_k, cache_mem_v, state_conv, state_lru_h, ffn1_norm, ffn1_w_in, ffn1_w_down, mix_norm, w_in, conv_w, conv_b, lru_w_a, lru_b_a, lru_w_i, lru_b_i, lru_lambda, sgu_ln, sgu_w, sgu_b, w_out, xa_norm, xa_w_q, xa_w_kv, xa_w_o, ffn2_norm, ffn2_w_in, ffn2_w_down):
    nbatch, seq, d = x_prompt.shape
    nb, ts, _ = x_sample.shape
    depth = ffn1_norm.shape[0]
    mem = mem_prompt.shape[1]
    dl = conv_w.shape[-1]
    ds = sgu_ln.shape[-1]
    hd = d // XA_HEADS
    d_ff = ffn1_w_down.shape[1]
    ff_pad = -(-d_ff // FF_TILE) * FF_TILE
    mp, ms = nbatch * seq, nb * ts
    assert seq % CHUNK == 0 and ts < CHUNK and ts >= CONV_WIDTH - 1
    assert nb % SAMPLE_GROUP == 0 and SAMPLE_GROUP * ts <= LANES
    assert dl % LRU_PAIR == 0 and ds % SGU_HEADS == 0

    tm = _pick_tile(math.gcd(mp, ms), (512, 256, 128, 64))
    tt = _pick_tile(seq, (256, 128))
    tmem = _pick_tile(nbatch * mem, (512, 256, 128))
    ngroups = nb // SAMPLE_GROUP

    x = jnp.concatenate([x_prompt.reshape(mp, d), jnp.transpose(x_sample, (1, 0, 2)).reshape(ms, d)], axis=0)
    memf = mem_prompt.reshape(nbatch * mem, d)
    k_cache = cache_mem_k.reshape(depth, nb, mem, d)
    v_cache = cache_mem_v.reshape(depth, nb, mem, d)

    def ffn_weights(w_i, w_d):
        wg = _pad_to(w_i[:, :d_ff], 1, ff_pad).astype(BF16)
        wu = _pad_to(w_i[:, d_ff:], 1, ff_pad).astype(BF16)
        wd = _pad_to(w_d, 0, ff_pad).astype(BF16)
        return wg, wu, wd

    outs = {k: [] for k in ("mk", "mv", "cvp", "hp", "cvs", "hs", "vs")}
    for l in range(depth):
        x = _ffn(x, ffn1_norm[l], *ffn_weights(ffn1_w_in[l], ffn1_w_down[l]), tm)

        z = _norm_matmul(x, mix_norm[l, 0:1], w_in[l].astype(BF16), tm, 1024, F32, "mix_in")
        wg, bg = _gate_weights(lru_w_a[l], lru_b_a[l], lru_w_i[l], lru_b_i[l])
        cw, cb, lam = conv_w[l], conv_b[l][None, :], lru_lambda[l][None, :]
        sb_rows = jnp.repeat(sgu_b[l].T, ds // SGU_HEADS, axis=1)
        y_p, xlast, hlast = _mixer_prompt(z, nbatch, seq, cw, cb, wg, bg, lam, sgu_ln[l], sgu_w[l], sb_rows, tt)
        tri = jnp.tril(jnp.ones((ts, ts), F32))
        swx = jnp.repeat(jnp.transpose(sgu_w[l][:, :ts, :ts] * tri, (1, 2, 0)).reshape(ts * ts, SGU_HEADS),
                         ds // SGU_HEADS, axis=1)
        y_s, conv_s, h_s, v_s = _mixer_sample(
            z[mp:], jnp.transpose(state_conv[l], (1, 0, 2)), state_lru_h[l], cw, cb, wg, bg, lam, sgu_ln[l],
            swx, sb_rows[:ts], nb, ts)
        y = jnp.concatenate([y_p, y_s], axis=0)
        x = _matmul_norm_res(y, w_out[l].astype(BF16), x, mix_norm[l, 1:2], tm, "mix_out")
        outs["cvp"].append(xlast[:, SUBLANES - (CONV_WIDTH - 1):, :])
        outs["hp"].append(hlast[:, 0, :])
        outs["cvs"].append(jnp.transpose(conv_s, (1, 0, 2)))
        outs["hs"].append(h_s)
        outs["vs"].append(jnp.transpose(v_s, (1, 0, 2)))

        kv = _norm_matmul(memf, xa_norm[l, 2:3], xa_w_kv[l].astype(BF16), tmem, 1024, F32, "mem_kv")
        k_mem, v_mem = kv[:, :d], kv[:, d:]
        outs["mk"].append(k_mem.reshape(nbatch, mem, XA_HEADS, hd))
        outs["mv"].append(v_mem.reshape(nbatch, mem, XA_HEADS, hd))
        q = _norm_matmul(x, xa_norm[l, 0:1], xa_w_q[l].astype(BF16), tm, 1024, BF16, "xa_q")
        kt = jnp.transpose(k_mem.astype(BF16).reshape(nbatch, mem, d), (0, 2, 1))
        o_p = _attn_prompt(q, kt, v_mem.astype(BF16).reshape(nbatch, mem, d), nbatch, seq, tm if seq % tm == 0 else tt)
        qt = q[mp:].reshape(ts, ngroups, SAMPLE_GROUP, XA_HEADS, hd)
        qt = jnp.transpose(qt, (1, 3, 4, 2, 0)).reshape(ngroups, XA_HEADS, hd, SAMPLE_GROUP * ts)
        qt = _pad_to(qt, 3, LANES)
        o_s = _attn_sample(qt, k_cache, v_cache, l, ts)
        o_s = jnp.transpose(o_s.reshape(ngroups, SAMPLE_GROUP, ts, d), (2, 0, 1, 3)).reshape(ms, d)
        o = jnp.concatenate([o_p, o_s], axis=0)
        x = _matmul_norm_res(o, xa_w_o[l].astype(BF16), x, xa_norm[l, 1:2], tm, "xa_out")

        x = _ffn(x, ffn2_norm[l], *ffn_weights(ffn2_w_in[l], ffn2_w_down[l]), tm)

    y_prompt = x[:mp].reshape(nbatch, seq, d)
    y_sample = jnp.transpose(x[mp:].reshape(ts, nb, d), (1, 0, 2))
    return (y_prompt, y_sample, jnp.stack(outs["mk"]), jnp.stack(outs["mv"]), jnp.stack(outs["cvp"]),
            jnp.stack(outs["hp"]), jnp.stack(outs["cvs"]), jnp.stack(outs["hs"]), jnp.stack(outs["vs"]))
```

```python
import functools
import math

import jax
import jax.numpy as jnp
from jax import lax
from jax.experimental import pallas as pl
from jax.experimental.pallas import tpu as pltpu

F32 = jnp.float32
BF16 = jnp.bfloat16

EPS = 1e-6
HALF = 0.5
LRU_C = 8.0
CONV_WIDTH = 4
CHUNK = 128
LRU_HEADS = 8
SGU_HEADS = 8
XA_HEADS = 4

VMEM_LIMIT_BYTES = 56 * 1024 * 1024
SUBLANES = 8
LANES = 128
FF_TILE = 1024
SAMPLE_GROUP = 8
CACHE_STEP_BATCH = 2
LRU_PAIR = 2 * LANES


def _params(*sem):
    return pltpu.CompilerParams(dimension_semantics=sem, vmem_limit_bytes=VMEM_LIMIT_BYTES)


def _pick_tile(n, prefs):
    for p in prefs:
        if n % p == 0:
            return p
    raise ValueError(f"no tile in {prefs} divides {n}")


def _rms(x, g):
    return (x * lax.rsqrt(jnp.mean(x * x, axis=-1, keepdims=True) + EPS)) * g


def _gelu(x):
    c = math.sqrt(2.0 / math.pi)
    return 0.5 * x * (1.0 + jnp.tanh(c * (x + 0.044715 * (x * x * x))))


def _softplus(x):
    return jnp.maximum(x, 0.0) + jnp.log1p(jnp.exp(-jnp.abs(x)))


def _ffn_kernel(x_ref, g_ref, wg_ref, wu_ref, wd_ref, o_ref, h_scr, *, nf, last_cols):
    f = pl.program_id(1)

    @pl.when(f == 0)
    def _():
        h_scr[...] = _rms(x_ref[...], g_ref[0:1, :]).astype(BF16)
        o_ref[...] = jnp.zeros_like(o_ref)

    def accumulate(cols):
        h = h_scr[...]
        gate = jnp.dot(h, wg_ref[:, :cols], preferred_element_type=F32)
        up = jnp.dot(h, wu_ref[:, :cols], preferred_element_type=F32)
        act = ((gate * jax.nn.sigmoid(gate)) * up).astype(BF16)
        o_ref[...] += jnp.dot(act, wd_ref[:cols, :], preferred_element_type=F32)

    if last_cols == FF_TILE:
        accumulate(FF_TILE)
    else:
        @pl.when(f < nf - 1)
        def _():
            accumulate(FF_TILE)

        @pl.when(f == nf - 1)
        def _():
            accumulate(last_cols)

    @pl.when(f == nf - 1)
    def _():
        o_ref[...] = x_ref[...] + HALF * _rms(o_ref[...], g_ref[1:2, :])


def _ffn(x, norm, wg, wu, wd, layer, tm):
    m, d = x.shape
    ff = wg.shape[2]
    nf = pl.cdiv(ff, FF_TILE)
    return pl.pallas_call(
        functools.partial(_ffn_kernel, nf=nf, last_cols=ff - (nf - 1) * FF_TILE),
        grid=(m // tm, nf),
        in_specs=[
            pl.BlockSpec((tm, d), lambda i, f: (i, 0)),
            pl.BlockSpec((None, 2, d), lambda i, f: (layer, 0, 0)),
            pl.BlockSpec((None, d, FF_TILE), lambda i, f: (layer, 0, f)),
            pl.BlockSpec((None, d, FF_TILE), lambda i, f: (layer, 0, f)),
            pl.BlockSpec((None, FF_TILE, d), lambda i, f: (layer, f, 0)),
        ],
        out_specs=pl.BlockSpec((tm, d), lambda i, f: (i, 0)),
        out_shape=jax.ShapeDtypeStruct((m, d), F32),
        scratch_shapes=[pltpu.VMEM((tm, d), BF16)],
        compiler_params=_params("parallel", "arbitrary"),
        name="ffn",
    )(x, norm, wg, wu, wd)


def _norm_matmul_kernel(x_ref, g_ref, w_ref, o_ref, h_scr):
    @pl.when(pl.program_id(1) == 0)
    def _():
        h_scr[...] = _rms(x_ref[...], g_ref[...]).astype(BF16)

    o_ref[...] = jnp.dot(h_scr[...], w_ref[...], preferred_element_type=F32).astype(o_ref.dtype)


def _norm_matmul(x, g, w, layer, tm, tn, out_dtype, name):
    m, d = x.shape
    n = w.shape[2]
    return pl.pallas_call(
        _norm_matmul_kernel,
        grid=(m // tm, n // tn),
        in_specs=[
            pl.BlockSpec((tm, d), lambda i, j: (i, 0)),
            pl.BlockSpec((1, d), lambda i, j: (0, 0)),
            pl.BlockSpec((None, d, tn), lambda i, j: (layer, 0, j)),
        ],
        out_specs=pl.BlockSpec((tm, tn), lambda i, j: (i, j)),
        out_shape=jax.ShapeDtypeStruct((m, n), out_dtype),
        scratch_shapes=[pltpu.VMEM((tm, d), BF16)],
        compiler_params=_params("parallel", "arbitrary"),
        name=name,
    )(x, g, w)


def _matmul_norm_res_kernel(y_ref, w_ref, x_ref, g_ref, o_ref):
    d = jnp.dot(y_ref[...], w_ref[...], preferred_element_type=F32)
    o_ref[...] = x_ref[...] + _rms(d, g_ref[...])


def _matmul_norm_res(y, w, layer, x, g, tm, name):
    m, k = y.shape
    n = w.shape[2]
    return pl.pallas_call(
        _matmul_norm_res_kernel,
        grid=(m // tm,),
        in_specs=[
            pl.BlockSpec((tm, k), lambda i: (i, 0)),
            pl.BlockSpec((None, k, n), lambda i: (layer, 0, 0)),
            pl.BlockSpec((tm, n), lambda i: (i, 0)),
            pl.BlockSpec((1, n), lambda i: (0, 0)),
        ],
        out_specs=pl.BlockSpec((tm, n), lambda i: (i, 0)),
        out_shape=jax.ShapeDtypeStruct((m, n), F32),
        compiler_params=_params("parallel"),
        name=name,
    )(y, w, x, g)


def _lru_gates(xc, wg, bg, c):
    g = jnp.dot(xc.astype(BF16), wg, preferred_element_type=F32) + bg
    r = jax.nn.sigmoid(g[:, :LRU_PAIR])
    i = jax.nn.sigmoid(g[:, LRU_PAIR:])
    log_a = c * r
    a = jnp.exp(log_a)
    b = jnp.sqrt(-jnp.tanh(log_a) * (a * a + 1.0)) * (i * xc)
    return a, b


def _layer_norm(x, g, b):
    mu = jnp.mean(x, axis=-1, keepdims=True)
    xc = x - mu
    return (xc * lax.rsqrt(jnp.mean(xc * xc, axis=-1, keepdims=True) + EPS)) * g + b


def _mixer_prompt_kernel(z_ref, cw_ref, cb_ref, wg_ref, bg_ref, lam_ref, ln_ref, sw_ref, sb_ref,
                         y_ref, xlast_ref, hlast_ref,
                         xprev_scr, hc_scr, xc_scr, a_scr, b_scr, h_scr, *, tt, dl, ds):
    t = pl.program_id(1)
    ng = tt // SUBLANES

    @pl.when(t == 0)
    def _():
        xprev_scr[...] = jnp.zeros_like(xprev_scr)
        hc_scr[...] = jnp.zeros_like(hc_scr)

    x = z_ref[:, 0:dl]
    w = cw_ref[...]
    cb = cb_ref[...]
    acc = cb + w[CONV_WIDTH - 1:CONV_WIDTH, :] * x
    for k in range(1, CONV_WIDTH):
        acc = acc + w[CONV_WIDTH - 1 - k:CONV_WIDTH - k, :] * pltpu.roll(x, k, 0)
    xc_scr[...] = acc
    x_head = x[0:SUBLANES, :]
    ext = jnp.concatenate([xprev_scr[...], x_head], axis=0)
    acc = cb + w[CONV_WIDTH - 1:CONV_WIDTH, :] * x_head
    for k in range(1, CONV_WIDTH):
        acc = acc + w[CONV_WIDTH - 1 - k:CONV_WIDTH - k, :] * pltpu.roll(ext, k, 0)[SUBLANES:, :]
    xc_scr[0:SUBLANES, :] = acc
    x_tail = x[tt - SUBLANES:tt, :]
    xprev_scr[...] = x_tail
    xlast_ref[0] = x_tail

    c_all = -LRU_C * _softplus(-lam_ref[...])
    sidx = lax.broadcasted_iota(jnp.int32, (ng, SUBLANES, LRU_PAIR), 1)
    for p in range(dl // LRU_PAIR):
        sl = slice(p * LRU_PAIR, (p + 1) * LRU_PAIR)
        a, b = _lru_gates(xc_scr[:, sl], wg_ref[p], bg_ref[p], c_all[:, sl])
        a = a.reshape(ng, SUBLANES, LRU_PAIR)
        b = b.reshape(ng, SUBLANES, LRU_PAIR)
        shift = 1
        while shift < SUBLANES:
            keep = sidx >= shift
            b = jnp.where(keep, a * pltpu.roll(b, shift, 1) + b, b)
            a = jnp.where(keep, a * pltpu.roll(a, shift, 1), a)
            shift *= 2
        a_scr[:, :, sl] = a
        b_scr[:, :, sl] = b

    def group_step(gi, carry):
        h = a_scr[gi] * carry + b_scr[gi]
        h_scr[gi] = h
        return h[SUBLANES - 1:SUBLANES, :]

    carry = lax.fori_loop(0, ng, group_step, hc_scr[...])
    hc_scr[...] = carry
    hlast_ref[0] = carry
    h_seq = h_scr[...].reshape(tt, dl)
    y_ref[:, 0:dl] = (h_seq * _gelu(z_ref[:, dl:2 * dl])).astype(BF16)

    u = _gelu(z_ref[:, 2 * dl:2 * dl + ds])
    v = _layer_norm(_gelu(z_ref[:, 2 * dl + ds:2 * dl + 2 * ds]), ln_ref[0:1, :], ln_ref[1:2, :])
    vb = v.astype(BF16)
    nc = tt // CHUNK
    hd = ds // SGU_HEADS
    row = lax.broadcasted_iota(jnp.int32, (CHUNK, CHUNK), 0)
    col = lax.broadcasted_iota(jnp.int32, (CHUNK, CHUNK), 1)
    for g in range(SGU_HEADS):
        cs = slice(g * hd, (g + 1) * hd)
        wt = jnp.where(row >= col, sw_ref[g], 0.0).astype(BF16)
        vg = jnp.concatenate([vb[c * CHUNK:(c + 1) * CHUNK, cs] for c in range(nc)], axis=1)
        s = jnp.dot(wt, vg, preferred_element_type=F32)
        for c in range(nc):
            rs = slice(c * CHUNK, (c + 1) * CHUNK)
            sc = s[:, c * hd:(c + 1) * hd] + sb_ref[:, cs]
            y_ref[rs, dl + g * hd:dl + (g + 1) * hd] = (u[rs, cs] * sc).astype(BF16)


def _mixer_prompt(z, out_rows, nbatch, seq, cw, cb, wg, bg, lam, ln, sw, sb, tt):
    dz = z.shape[1]
    dl = cw.shape[1]
    ds = ln.shape[1]
    nt = seq // tt
    ng = tt // SUBLANES
    full = lambda *shape: pl.BlockSpec(shape, lambda b, t: (0,) * len(shape))
    return pl.pallas_call(
        functools.partial(_mixer_prompt_kernel, tt=tt, dl=dl, ds=ds),
        grid=(nbatch, nt),
        in_specs=[
            pl.BlockSpec((tt, dz), lambda b, t: (b * nt + t, 0)),
            full(*cw.shape), full(*cb.shape), full(*wg.shape), full(*bg.shape), full(*lam.shape),
            full(*ln.shape), full(*sw.shape), full(*sb.shape),
        ],
        out_specs=[
            pl.BlockSpec((tt, dl + ds), lambda b, t: (b * nt + t, 0)),
            pl.BlockSpec((1, SUBLANES, dl), lambda b, t: (b, 0, 0)),
            pl.BlockSpec((1, 1, dl), lambda b, t: (b, 0, 0)),
        ],
        out_shape=[
            jax.ShapeDtypeStruct((out_rows, dl + ds), BF16),
            jax.ShapeDtypeStruct((nbatch, SUBLANES, dl), F32),
            jax.ShapeDtypeStruct((nbatch, 1, dl), F32),
        ],
        scratch_shapes=[
            pltpu.VMEM((SUBLANES, dl), F32),
            pltpu.VMEM((1, dl), F32),
            pltpu.VMEM((tt, dl), F32),
            pltpu.VMEM((ng, SUBLANES, dl), F32),
            pltpu.VMEM((ng, SUBLANES, dl), F32),
            pltpu.VMEM((ng, SUBLANES, dl), F32),
        ],
        compiler_params=_params("arbitrary", "arbitrary"),
        name="mixer_prompt",
    )(z, cw, cb, wg, bg, lam, ln, sw, sb)


def _mixer_sample_kernel(z_ref, cbuf_ref, h0_ref, cw_ref, cb_ref, wg_ref, bg_ref, lam_ref, ln_ref,
                         swx_ref, sbx_ref,
                         y_ref, convnew_ref, hlast_ref, v_ref, xc_scr, *, nb, ts, dl, ds):
    rows = lambda t: slice(t * nb, (t + 1) * nb)
    hist = CONV_WIDTH - 1
    xx = [cbuf_ref[k] for k in range(hist)] + [z_ref[rows(t), 0:dl] for t in range(ts)]
    w = cw_ref[...]
    cb = cb_ref[...]
    for t in range(ts):
        acc = cb + w[0:1, :] * xx[t]
        for k in range(1, CONV_WIDTH):
            acc = acc + w[k:k + 1, :] * xx[t + k]
        xc_scr[rows(t), :] = acc
    for k in range(hist):
        convnew_ref[k] = xx[ts + k]

    c_all = -LRU_C * _softplus(-lam_ref[...])
    for p in range(dl // LRU_PAIR):
        sl = slice(p * LRU_PAIR, (p + 1) * LRU_PAIR)
        a, b = _lru_gates(xc_scr[:, sl], wg_ref[p], bg_ref[p], c_all[:, sl])
        h = h0_ref[:, sl]
        for t in range(ts):
            h = a[rows(t), :] * h + b[rows(t), :]
            gl = z_ref[rows(t), dl + p * LRU_PAIR:dl + (p + 1) * LRU_PAIR]
            y_ref[rows(t), sl] = (h * _gelu(gl)).astype(BF16)
        hlast_ref[:, sl] = h

    vs = []
    for t in range(ts):
        v = _layer_norm(_gelu(z_ref[rows(t), 2 * dl + ds:2 * dl + 2 * ds]), ln_ref[0:1, :], ln_ref[1:2, :])
        v_ref[t] = v
        vs.append(v)
    for t in range(ts):
        s = sbx_ref[t:t + 1, :] + swx_ref[t * ts:t * ts + 1, :] * vs[0]
        for j in range(1, t + 1):
            s = s + swx_ref[t * ts + j:t * ts + j + 1, :] * vs[j]
        u = _gelu(z_ref[rows(t), 2 * dl:2 * dl + ds])
        y_ref[rows(t), dl:dl + ds] = (u * s).astype(BF16)


def _mixer_sample(z, cbuf, h0, cw, cb, wg, bg, lam, ln, swx, sbx, nb, ts):
    dl = cw.shape[1]
    ds = ln.shape[1]
    return pl.pallas_call(
        functools.partial(_mixer_sample_kernel, nb=nb, ts=ts, dl=dl, ds=ds),
        out_shape=[
            jax.ShapeDtypeStruct((ts * nb, dl + ds), BF16),
            jax.ShapeDtypeStruct((CONV_WIDTH - 1, nb, dl), F32),
            jax.ShapeDtypeStruct((nb, dl), F32),
            jax.ShapeDtypeStruct((ts, nb, ds), F32),
        ],
        scratch_shapes=[pltpu.VMEM((ts * nb, dl), F32)],
        compiler_params=pltpu.CompilerParams(vmem_limit_bytes=VMEM_LIMIT_BYTES),
        name="mixer_sample",
    )(z, cbuf, h0, cw, cb, wg, bg, lam, ln, swx, sbx)


def _softmax_rows(s):
    e = jnp.exp(s - jnp.max(s, axis=-1, keepdims=True))
    return e / jnp.sum(e, axis=-1, keepdims=True)


def _attn_prompt_kernel(q_ref, kt_ref, v_ref, o_ref, *, hd, scale):
    for h in range(XA_HEADS):
        hs = slice(h * hd, (h + 1) * hd)
        s = jnp.dot(q_ref[:, hs], kt_ref[0, hs, :], preferred_element_type=F32) * scale
        p = _softmax_rows(s).astype(BF16)
        o_ref[:, hs] = jnp.dot(p, v_ref[0, :, hs], preferred_element_type=F32).astype(BF16)


def _attn_prompt(q, kt, v, out_rows, nbatch, seq, tq):
    d = q.shape[1]
    mem = v.shape[1]
    hd = d // XA_HEADS
    nt = seq // tq
    return pl.pallas_call(
        functools.partial(_attn_prompt_kernel, hd=hd, scale=hd ** -0.5),
        grid=(nbatch, nt),
        in_specs=[
            pl.BlockSpec((tq, d), lambda b, t: (b * nt + t, 0)),
            pl.BlockSpec((1, d, mem), lambda b, t: (b, 0, 0)),
            pl.BlockSpec((1, mem, d), lambda b, t: (b, 0, 0)),
        ],
        out_specs=pl.BlockSpec((tq, d), lambda b, t: (b * nt + t, 0)),
        out_shape=jax.ShapeDtypeStruct((out_rows, d), BF16),
        compiler_params=_params("parallel", "parallel"),
        name="attn_prompt",
    )(q, kt, v)


def _attn_sample_kernel(qt_ref, k_ref, v_ref, o_ref, *, ts, scale, steps_per_group):
    gb, rows, _ = k_ref.shape
    nh, hd, _ = qt_ref.shape[1:]
    nchunk = hd // LANES
    per_token = nh * nchunk
    mem = rows // per_token
    sub = pl.program_id(0) % steps_per_group

    @pl.when(sub == 0)
    def _():
        o_ref[...] = jnp.zeros_like(o_ref)

    def head_rows(ref, h):
        return jnp.concatenate(
            [jnp.concatenate([ref[j, pl.ds(c * nh + h, mem, stride=per_token), :] for c in range(nchunk)], axis=1)
             for j in range(gb)], axis=0).astype(BF16)

    lane_owner = lax.broadcasted_iota(jnp.int32, (gb, mem, LANES), 2) // ts
    own = lane_owner == sub * gb + lax.broadcasted_iota(jnp.int32, (gb, mem, LANES), 0)
    for h in range(nh):
        st = jnp.dot(head_rows(k_ref, h), qt_ref[0, h], preferred_element_type=F32) * scale
        st = st.reshape(gb, mem, LANES)
        e = jnp.exp(st - jnp.max(st, axis=1, keepdims=True))
        p = e / jnp.sum(e, axis=1, keepdims=True)
        pm = jnp.where(own, p, 0.0).astype(BF16).reshape(gb * mem, LANES)
        o = lax.dot_general(pm, head_rows(v_ref, h), (((0,), (0,)), ((), ())), preferred_element_type=F32)
        o_ref[0, :, h * hd:(h + 1) * hd] += o[0:o_ref.shape[1], :]


def _attn_sample(qt, k_rows, v_rows, layer, ts):
    ngroups, nh, hd, _ = qt.shape
    _, nb, rows, _ = k_rows.shape
    gb = CACHE_STEP_BATCH
    steps_per_group = nb // ngroups // gb
    kv_spec = pl.BlockSpec((None, gb, rows, LANES), lambda i: (layer, i, 0, 0))
    return pl.pallas_call(
        functools.partial(_attn_sample_kernel, ts=ts, scale=hd ** -0.5, steps_per_group=steps_per_group),
        grid=(nb // gb,),
        in_specs=[
            pl.BlockSpec((1, nh, hd, LANES), lambda i: (i // steps_per_group, 0, 0, 0)),
            kv_spec,
            kv_spec,
        ],
        out_specs=pl.BlockSpec((1, SAMPLE_GROUP * ts, nh * hd), lambda i: (i // steps_per_group, 0, 0)),
        out_shape=jax.ShapeDtypeStruct((ngroups, SAMPLE_GROUP * ts, nh * hd), F32),
        compiler_params=_params("arbitrary"),
        name="attn_sample",
    )(qt, k_rows, v_rows)


def _cache_rows(cache):
    depth, nb, mem, nh, hd = cache.shape
    c = cache.reshape(depth, nb, mem, nh, hd // LANES, LANES)
    return jnp.transpose(c, (0, 1, 2, 4, 3, 5)).reshape(depth, nb, mem * nh * (hd // LANES), LANES)


def _pad_to(a, axis, size):
    pad = [(0, 0)] * a.ndim
    pad[axis] = (0, size - a.shape[axis])
    return jnp.pad(a, pad)


def _gate_weights(w_a, b_a, w_i, b_i):
    heads, hd, _ = w_a.shape
    per = LRU_PAIR // hd
    npair = heads // per

    def blockdiag(w):
        w = w.reshape(npair, per, hd, hd)
        eye = jnp.eye(per, dtype=w.dtype)
        return jnp.einsum('pade,ab->padbe', w, eye).reshape(npair, per * hd, per * hd)

    wg = jnp.concatenate([blockdiag(w_a), blockdiag(w_i)], axis=-1).astype(BF16)
    bg = jnp.concatenate([b_a.reshape(npair, 1, per * hd), b_i.reshape(npair, 1, per * hd)], axis=-1)
    return wg, bg


def kernel(x_prompt, x_sample, mem_prompt, cache_mem_k, cache_mem_v, state_conv, state_lru_h, ffn1_norm, ffn1_w_in, ffn1_w_down, mix_norm, w_in, conv_w, conv_b, lru_w_a, lru_b_a, lru_w_i, lru_b_i, lru_lambda, sgu_ln, sgu_w, sgu_b, w_out, xa_norm, xa_w_q, xa_w_kv, xa_w_o, ffn2_norm, ffn2_w_in, ffn2_w_down):
    nbatch, seq, d = x_prompt.shape
    nb, ts, _ = x_sample.shape
    depth = ffn1_norm.shape[0]
    mem = mem_prompt.shape[1]
    dl = conv_w.shape[-1]
    ds = sgu_ln.shape[-1]
    hd = d // XA_HEADS
    d_ff = ffn1_w_down.shape[1]
    mp, ms = nbatch * seq, nb * ts
    mt = mp + ms
    assert seq % CHUNK == 0 and ts < CHUNK and ts >= CONV_WIDTH - 1
    assert nb % SAMPLE_GROUP == 0 and SAMPLE_GROUP * ts <= LANES
    assert dl % LRU_PAIR == 0 and ds % SGU_HEADS == 0

    tm = _pick_tile(math.gcd(mp, ms), (512, 256, 128, 64))
    tt = _pick_tile(seq, (256, 128))
    tmem = _pick_tile(nbatch * mem, (512, 256, 128))
    ngroups = nb // SAMPLE_GROUP

    x = jnp.concatenate([x_prompt.reshape(mp, d), jnp.transpose(x_sample, (1, 0, 2)).reshape(ms, d)], axis=0)
    memf = mem_prompt.reshape(nbatch * mem, d)

    ffn1_wg, ffn1_wu = ffn1_w_in[:, :, :d_ff].astype(BF16), ffn1_w_in[:, :, d_ff:].astype(BF16)
    ffn2_wg, ffn2_wu = ffn2_w_in[:, :, :d_ff].astype(BF16), ffn2_w_in[:, :, d_ff:].astype(BF16)
    ffn1_wd, ffn2_wd = ffn1_w_down.astype(BF16), ffn2_w_down.astype(BF16)
    w_in_b, w_out_b = w_in.astype(BF16), w_out.astype(BF16)
    w_q_b, w_kv_b, w_o_b = xa_w_q.astype(BF16), xa_w_kv.astype(BF16), xa_w_o.astype(BF16)
    k_rows, v_rows = _cache_rows(cache_mem_k), _cache_rows(cache_mem_v)

    outs = {k: [] for k in ("mk", "mv", "cvp", "hp", "cvs", "hs", "vs")}
    for l in range(depth):
        x = _ffn(x, ffn1_norm, ffn1_wg, ffn1_wu, ffn1_wd, l, tm)

        z = _norm_matmul(x, mix_norm[l, 0:1], w_in_b, l, tm, _pick_tile(w_in.shape[2], (2048, 1024)), F32, "mix_in")
        wg, bg = _gate_weights(lru_w_a[l], lru_b_a[l], lru_w_i[l], lru_b_i[l])
        cw, cb, lam = conv_w[l], conv_b[l][None, :], lru_lambda[l][None, :]
        sb_rows = jnp.repeat(sgu_b[l].T, ds // SGU_HEADS, axis=1)
        y, xlast, hlast = _mixer_prompt(z, mt, nbatch, seq, cw, cb, wg, bg, lam, sgu_ln[l], sgu_w[l], sb_rows, tt)
        tri = jnp.tril(jnp.ones((ts, ts), F32))
        swx = jnp.repeat(jnp.transpose(sgu_w[l][:, :ts, :ts] * tri, (1, 2, 0)).reshape(ts * ts, SGU_HEADS),
                         ds // SGU_HEADS, axis=1)
        y_s, conv_s, h_s, v_s = _mixer_sample(
            z[mp:], jnp.transpose(state_conv[l], (1, 0, 2)), state_lru_h[l], cw, cb, wg, bg, lam, sgu_ln[l],
            swx, sb_rows[:ts], nb, ts)
        y = lax.dynamic_update_slice(y, y_s, (mp, 0))
        x = _matmul_norm_res(y, w_out_b, l, x, mix_norm[l, 1:2], tm, "mix_out")
        outs["cvp"].append(xlast[:, SUBLANES - (CONV_WIDTH - 1):, :])
        outs["hp"].append(hlast[:, 0, :])
        outs["cvs"].append(jnp.transpose(conv_s, (1, 0, 2)))
        outs["hs"].append(h_s)
        outs["vs"].append(jnp.transpose(v_s, (1, 0, 2)))

        kv = _norm_matmul(memf, xa_norm[l, 2:3], w_kv_b, l, tmem, 1024, F32, "mem_kv")
        k_mem, v_mem = kv[:, :d], kv[:, d:]
        outs["mk"].append(k_mem.reshape(nbatch, mem, XA_HEADS, hd))
        outs["mv"].append(v_mem.reshape(nbatch, mem, XA_HEADS, hd))
        q = _norm_matmul(x, xa_norm[l, 0:1], w_q_b, l, tm, d, BF16, "xa_q")
        kt = jnp.transpose(k_mem.astype(BF16).reshape(nbatch, mem, d), (0, 2, 1))
        o = _attn_prompt(q, kt, v_mem.astype(BF16).reshape(nbatch, mem, d), mt, nbatch, seq,
                         tm if seq % tm == 0 else tt)
        qt = q[mp:].reshape(ts, ngroups, SAMPLE_GROUP, XA_HEADS, hd)
        qt = jnp.transpose(qt, (1, 3, 4, 2, 0)).reshape(ngroups, XA_HEADS, hd, SAMPLE_GROUP * ts)
        qt = _pad_to(qt, 3, LANES)
        o_s = _attn_sample(qt, k_rows, v_rows, l, ts).astype(BF16)
        o_s = jnp.transpose(o_s.reshape(ngroups, SAMPLE_GROUP, ts, d), (2, 0, 1, 3)).reshape(ms, d)
        o = lax.dynamic_update_slice(o, o_s, (mp, 0))
        x = _matmul_norm_res(o, w_o_b, l, x, xa_norm[l, 1:2], tm, "xa_out")

        x = _ffn(x, ffn2_norm, ffn2_wg, ffn2_wu, ffn2_wd, l, tm)

    y_prompt = x[:mp].reshape(nbatch, seq, d)
    y_sample = jnp.transpose(x[mp:].reshape(ts, nb, d), (1, 0, 2))
    return (y_prompt, y_sample, jnp.stack(outs["mk"]), jnp.stack(outs["mv"]), jnp.stack(outs["cvp"]),
            jnp.stack(outs["hp"]), jnp.stack(outs["cvs"]), jnp.stack(outs["hs"]), jnp.stack(outs["vs"]))
```

```python
import functools
import math

import jax
import jax.numpy as jnp
from jax import lax
from jax.experimental import pallas as pl
from jax.experimental.pallas import tpu as pltpu

F32 = jnp.float32
BF16 = jnp.bfloat16

EPS = 1e-6
HALF = 0.5
LRU_C = 8.0
CONV_WIDTH = 4
CHUNK = 128
LRU_HEADS = 8
SGU_HEADS = 8
XA_HEADS = 4

VMEM_LIMIT_BYTES = 56 * 1024 * 1024
SUBLANES = 8
LANES = 128
FF_TILE = 1024
SAMPLE_GROUP = 8
CACHE_STEP_BATCH = 4
LRU_PAIR = 2 * LANES


def _params(*sem):
    return pltpu.CompilerParams(dimension_semantics=sem, vmem_limit_bytes=VMEM_LIMIT_BYTES)


def _pick_tile(n, prefs):
    for p in prefs:
        if n % p == 0:
            return p
    raise ValueError(f"no tile in {prefs} divides {n}")


def _rms(x, g):
    return (x * lax.rsqrt(jnp.mean(x * x, axis=-1, keepdims=True) + EPS)) * g


def _gelu(x):
    c = math.sqrt(2.0 / math.pi)
    return 0.5 * x * (1.0 + jnp.tanh(c * (x + 0.044715 * (x * x * x))))


def _softplus(x):
    return jnp.maximum(x, 0.0) + jnp.log1p(jnp.exp(-jnp.abs(x)))


def _ffn_kernel(x_ref, g_ref, wg_ref, wu_ref, wd_ref, o_ref, h_scr, *, nf, last_cols):
    f = pl.program_id(1)

    @pl.when(f == 0)
    def _():
        h_scr[...] = _rms(x_ref[...], g_ref[0:1, :]).astype(BF16)
        o_ref[...] = jnp.zeros_like(o_ref)

    def accumulate(cols):
        h = h_scr[...]
        gate = jnp.dot(h, wg_ref[:, :cols], preferred_element_type=F32)
        up = jnp.dot(h, wu_ref[:, :cols], preferred_element_type=F32)
        act = ((gate * jax.nn.sigmoid(gate)) * up).astype(BF16)
        o_ref[...] += jnp.dot(act, wd_ref[:cols, :], preferred_element_type=F32)

    if last_cols == FF_TILE:
        accumulate(FF_TILE)
    else:
        @pl.when(f < nf - 1)
        def _():
            accumulate(FF_TILE)

        @pl.when(f == nf - 1)
        def _():
            accumulate(last_cols)

    @pl.when(f == nf - 1)
    def _():
        o_ref[...] = x_ref[...] + HALF * _rms(o_ref[...], g_ref[1:2, :])


def _ffn(x, norm, wgu, wd, layer, tm):
    m, d = x.shape
    ff = wgu.shape[3]
    nf = pl.cdiv(ff, FF_TILE)
    return pl.pallas_call(
        functools.partial(_ffn_kernel, nf=nf, last_cols=ff - (nf - 1) * FF_TILE),
        grid=(m // tm, nf),
        in_specs=[
            pl.BlockSpec((tm, d), lambda i, f: (i, 0)),
            pl.BlockSpec((None, 2, d), lambda i, f: (layer, 0, 0)),
            pl.BlockSpec((None, None, d, FF_TILE), lambda i, f: (layer, 0, 0, f)),
            pl.BlockSpec((None, None, d, FF_TILE), lambda i, f: (layer, 1, 0, f)),
            pl.BlockSpec((None, FF_TILE, d), lambda i, f: (layer, f, 0)),
        ],
        out_specs=pl.BlockSpec((tm, d), lambda i, f: (i, 0)),
        out_shape=jax.ShapeDtypeStruct((m, d), F32),
        scratch_shapes=[pltpu.VMEM((tm, d), BF16)],
        compiler_params=_params("parallel", "arbitrary"),
        name="ffn",
    )(x, norm, wgu, wgu, wd)


def _norm_matmul_kernel(x_ref, g_ref, w_ref, o_ref, h_scr):
    @pl.when(pl.program_id(1) == 0)
    def _():
        h_scr[...] = _rms(x_ref[...], g_ref[...]).astype(BF16)

    o_ref[...] = jnp.dot(h_scr[...], w_ref[...], preferred_element_type=F32).astype(o_ref.dtype)


def _norm_matmul(x, g, w, layer, tm, tn, out_dtype, name):
    m, d = x.shape
    n = w.shape[2]
    return pl.pallas_call(
        _norm_matmul_kernel,
        grid=(m // tm, n // tn),
        in_specs=[
            pl.BlockSpec((tm, d), lambda i, j: (i, 0)),
            pl.BlockSpec((1, d), lambda i, j: (0, 0)),
            pl.BlockSpec((None, d, tn), lambda i, j: (layer, 0, j)),
        ],
        out_specs=pl.BlockSpec((tm, tn), lambda i, j: (i, j)),
        out_shape=jax.ShapeDtypeStruct((m, n), out_dtype),
        scratch_shapes=[pltpu.VMEM((tm, d), BF16)],
        compiler_params=_params("parallel", "arbitrary"),
        name=name,
    )(x, g, w)


def _matmul_norm_res_kernel(y_ref, w_ref, x_ref, g_ref, o_ref):
    d = jnp.dot(y_ref[...], w_ref[...], preferred_element_type=F32)
    o_ref[...] = x_ref[...] + _rms(d, g_ref[...])


def _matmul_norm_res(y, w, layer, x, g, tm, name):
    m, k = y.shape
    n = w.shape[2]
    return pl.pallas_call(
        _matmul_norm_res_kernel,
        grid=(m // tm,),
        in_specs=[
            pl.BlockSpec((tm, k), lambda i: (i, 0)),
            pl.BlockSpec((None, k, n), lambda i: (layer, 0, 0)),
            pl.BlockSpec((tm, n), lambda i: (i, 0)),
            pl.BlockSpec((1, n), lambda i: (0, 0)),
        ],
        out_specs=pl.BlockSpec((tm, n), lambda i: (i, 0)),
        out_shape=jax.ShapeDtypeStruct((m, n), F32),
        compiler_params=_params("parallel"),
        name=name,
    )(y, w, x, g)


def _lru_gates(xc, wg, bg, c):
    g = jnp.dot(xc.astype(BF16), wg, preferred_element_type=F32) + bg
    r = jax.nn.sigmoid(g[:, :LRU_PAIR])
    i = jax.nn.sigmoid(g[:, LRU_PAIR:])
    log_a = c * r
    a = jnp.exp(log_a)
    b = jnp.sqrt(-jnp.tanh(log_a) * (a * a + 1.0)) * (i * xc)
    return a, b


def _layer_norm(x, g, b):
    mu = jnp.mean(x, axis=-1, keepdims=True)
    xc = x - mu
    return (xc * lax.rsqrt(jnp.mean(xc * xc, axis=-1, keepdims=True) + EPS)) * g + b


def _mixer_prompt_kernel(z_ref, cw_ref, cb_ref, wg_ref, bg_ref, lam_ref, ln_ref, sw_ref, sb_ref, ys_ref,
                         y_ref, xlast_ref, hlast_ref, *scratch, nt, n_prompt, tt, dl, ds):
    i = pl.program_id(0)

    @pl.when(i < n_prompt)
    def _():
        _mixer_prompt_tile(i % nt, z_ref, cw_ref, cb_ref, wg_ref, bg_ref, lam_ref, ln_ref, sw_ref, sb_ref,
                           y_ref, xlast_ref, hlast_ref, *scratch, tt=tt, dl=dl, ds=ds)

    @pl.when(i >= n_prompt)
    def _():
        y_ref[...] = ys_ref[...]


def _mixer_prompt_tile(t, z_ref, cw_ref, cb_ref, wg_ref, bg_ref, lam_ref, ln_ref, sw_ref, sb_ref,
                       y_ref, xlast_ref, hlast_ref,
                       xprev_scr, hc_scr, xc_scr, a_scr, b_scr, h_scr, *, tt, dl, ds):
    ng = tt // SUBLANES

    @pl.when(t == 0)
    def _():
        xprev_scr[...] = jnp.zeros_like(xprev_scr)
        hc_scr[...] = jnp.zeros_like(hc_scr)

    x = z_ref[:, 0:dl]
    w = cw_ref[...]
    cb = cb_ref[...]
    acc = cb + w[CONV_WIDTH - 1:CONV_WIDTH, :] * x
    for k in range(1, CONV_WIDTH):
        acc = acc + w[CONV_WIDTH - 1 - k:CONV_WIDTH - k, :] * pltpu.roll(x, k, 0)
    xc_scr[...] = acc
    x_head = x[0:SUBLANES, :]
    ext = jnp.concatenate([xprev_scr[...], x_head], axis=0)
    acc = cb + w[CONV_WIDTH - 1:CONV_WIDTH, :] * x_head
    for k in range(1, CONV_WIDTH):
        acc = acc + w[CONV_WIDTH - 1 - k:CONV_WIDTH - k, :] * pltpu.roll(ext, k, 0)[SUBLANES:, :]
    xc_scr[0:SUBLANES, :] = acc
    x_tail = x[tt - SUBLANES:tt, :]
    xprev_scr[...] = x_tail
    xlast_ref[0] = x_tail

    c_all = -LRU_C * _softplus(-lam_ref[...])
    sidx = lax.broadcasted_iota(jnp.int32, (ng, SUBLANES, LRU_PAIR), 1)
    for p in range(dl // LRU_PAIR):
        sl = slice(p * LRU_PAIR, (p + 1) * LRU_PAIR)
        a, b = _lru_gates(xc_scr[:, sl], wg_ref[p], bg_ref[p], c_all[:, sl])
        a = a.reshape(ng, SUBLANES, LRU_PAIR)
        b = b.reshape(ng, SUBLANES, LRU_PAIR)
        shift = 1
        while shift < SUBLANES:
            keep = sidx >= shift
            b = jnp.where(keep, a * pltpu.roll(b, shift, 1) + b, b)
            a = jnp.where(keep, a * pltpu.roll(a, shift, 1), a)
            shift *= 2
        a_scr[:, :, sl] = a
        b_scr[:, :, sl] = b

    def group_step(gi, carry):
        h = a_scr[gi] * carry + b_scr[gi]
        h_scr[gi] = h
        return h[SUBLANES - 1:SUBLANES, :]

    carry = lax.fori_loop(0, ng, group_step, hc_scr[...])
    hc_scr[...] = carry
    hlast_ref[0] = carry
    h_seq = h_scr[...].reshape(tt, dl)
    y_ref[:, 0:dl] = (h_seq * _gelu(z_ref[:, dl:2 * dl])).astype(BF16)

    u = _gelu(z_ref[:, 2 * dl:2 * dl + ds])
    v = _layer_norm(_gelu(z_ref[:, 2 * dl + ds:2 * dl + 2 * ds]), ln_ref[0:1, :], ln_ref[1:2, :])
    vb = v.astype(BF16)
    nc = tt // CHUNK
    hd = ds // SGU_HEADS
    row = lax.broadcasted_iota(jnp.int32, (CHUNK, CHUNK), 0)
    col = lax.broadcasted_iota(jnp.int32, (CHUNK, CHUNK), 1)
    for g in range(SGU_HEADS):
        cs = slice(g * hd, (g + 1) * hd)
        wt = jnp.where(row >= col, sw_ref[g], 0.0).astype(BF16)
        vg = jnp.concatenate([vb[c * CHUNK:(c + 1) * CHUNK, cs] for c in range(nc)], axis=1)
        s = jnp.dot(wt, vg, preferred_element_type=F32)
        for c in range(nc):
            rs = slice(c * CHUNK, (c + 1) * CHUNK)
            sc = s[:, c * hd:(c + 1) * hd] + sb_ref[:, cs]
            y_ref[rs, dl + g * hd:dl + (g + 1) * hd] = (u[rs, cs] * sc).astype(BF16)


def _mixer_prompt(z, ys, nbatch, seq, cw, cb, wg, bg, lam, ln, sw, sb, tt):
    dz = z.shape[1]
    dl = cw.shape[1]
    ds = ln.shape[1]
    nt = seq // tt
    n_prompt = nbatch * nt
    n_sample = ys.shape[0] // tt
    ng = tt // SUBLANES
    full = lambda *shape: pl.BlockSpec(shape, lambda i: (0,) * len(shape))
    batch_of = lambda i: jnp.minimum(i // nt, nbatch - 1)
    return pl.pallas_call(
        functools.partial(_mixer_prompt_kernel, nt=nt, n_prompt=n_prompt, tt=tt, dl=dl, ds=ds),
        grid=(n_prompt + n_sample,),
        in_specs=[
            pl.BlockSpec((tt, dz), lambda i: (jnp.minimum(i, n_prompt - 1), 0)),
            full(*cw.shape), full(*cb.shape), full(*wg.shape), full(*bg.shape), full(*lam.shape),
            full(*ln.shape), full(*sw.shape), full(*sb.shape),
            pl.BlockSpec((tt, dl + ds), lambda i: (jnp.maximum(i - n_prompt, 0), 0)),
        ],
        out_specs=[
            pl.BlockSpec((tt, dl + ds), lambda i: (i, 0)),
            pl.BlockSpec((1, SUBLANES, dl), lambda i: (batch_of(i), 0, 0)),
            pl.BlockSpec((1, 1, dl), lambda i: (batch_of(i), 0, 0)),
        ],
        out_shape=[
            jax.ShapeDtypeStruct((nbatch * seq + ys.shape[0], dl + ds), BF16),
            jax.ShapeDtypeStruct((nbatch, SUBLANES, dl), F32),
            jax.ShapeDtypeStruct((nbatch, 1, dl), F32),
        ],
        scratch_shapes=[
            pltpu.VMEM((SUBLANES, dl), F32),
            pltpu.VMEM((1, dl), F32),
            pltpu.VMEM((tt, dl), F32),
            pltpu.VMEM((ng, SUBLANES, dl), F32),
            pltpu.VMEM((ng, SUBLANES, dl), F32),
            pltpu.VMEM((ng, SUBLANES, dl), F32),
        ],
        compiler_params=_params("arbitrary"),
        name="mixer_prompt",
    )(z, cw, cb, wg, bg, lam, ln, sw, sb, ys)


def _mixer_sample_kernel(z_ref, cbuf_ref, h0_ref, cw_ref, cb_ref, wg_ref, bg_ref, lam_ref, ln_ref,
                         swx_ref, sbx_ref,
                         y_ref, convnew_ref, hlast_ref, v_ref, xc_scr, *, nb, ts, dl, ds):
    rows = lambda t: slice(t * nb, (t + 1) * nb)
    hist = CONV_WIDTH - 1
    xx = [cbuf_ref[k] for k in range(hist)] + [z_ref[rows(t), 0:dl] for t in range(ts)]
    w = cw_ref[...]
    cb = cb_ref[...]
    for t in range(ts):
        acc = cb + w[0:1, :] * xx[t]
        for k in range(1, CONV_WIDTH):
            acc = acc + w[k:k + 1, :] * xx[t + k]
        xc_scr[rows(t), :] = acc
    for k in range(hist):
        convnew_ref[k] = xx[ts + k]

    c_all = -LRU_C * _softplus(-lam_ref[...])
    for p in range(dl // LRU_PAIR):
        sl = slice(p * LRU_PAIR, (p + 1) * LRU_PAIR)
        a, b = _lru_gates(xc_scr[:, sl], wg_ref[p], bg_ref[p], c_all[:, sl])
        h = h0_ref[:, sl]
        for t in range(ts):
            h = a[rows(t), :] * h + b[rows(t), :]
            gl = z_ref[rows(t), dl + p * LRU_PAIR:dl + (p + 1) * LRU_PAIR]
            y_ref[rows(t), sl] = (h * _gelu(gl)).astype(BF16)
        hlast_ref[:, sl] = h

    vs = []
    for t in range(ts):
        v = _layer_norm(_gelu(z_ref[rows(t), 2 * dl + ds:2 * dl + 2 * ds]), ln_ref[0:1, :], ln_ref[1:2, :])
        v_ref[t] = v
        vs.append(v)
    for t in range(ts):
        s = sbx_ref[t:t + 1, :] + swx_ref[t * ts:t * ts + 1, :] * vs[0]
        for j in range(1, t + 1):
            s = s + swx_ref[t * ts + j:t * ts + j + 1, :] * vs[j]
        u = _gelu(z_ref[rows(t), 2 * dl:2 * dl + ds])
        y_ref[rows(t), dl:dl + ds] = (u * s).astype(BF16)


def _mixer_sample(z, cbuf, h0, cw, cb, wg, bg, lam, ln, swx, sbx, nb, ts):
    dl = cw.shape[1]
    ds = ln.shape[1]
    return pl.pallas_call(
        functools.partial(_mixer_sample_kernel, nb=nb, ts=ts, dl=dl, ds=ds),
        out_shape=[
            jax.ShapeDtypeStruct((ts * nb, dl + ds), BF16),
            jax.ShapeDtypeStruct((CONV_WIDTH - 1, nb, dl), F32),
            jax.ShapeDtypeStruct((nb, dl), F32),
            jax.ShapeDtypeStruct((ts, nb, ds), F32),
        ],
        scratch_shapes=[pltpu.VMEM((ts * nb, dl), F32)],
        compiler_params=pltpu.CompilerParams(vmem_limit_bytes=VMEM_LIMIT_BYTES),
        name="mixer_sample",
    )(z, cbuf, h0, cw, cb, wg, bg, lam, ln, swx, sbx)


def _softmax_rows(s):
    e = jnp.exp(s - jnp.max(s, axis=-1, keepdims=True))
    return e / jnp.sum(e, axis=-1, keepdims=True)


def _attn_prompt_kernel(q_ref, kt_ref, v_ref, os_ref, o_ref, *, n_prompt, hd, scale):
    i = pl.program_id(0)

    @pl.when(i < n_prompt)
    def _():
        for h in range(XA_HEADS):
            hs = slice(h * hd, (h + 1) * hd)
            s = jnp.dot(q_ref[:, hs], kt_ref[0, hs, :], preferred_element_type=F32) * scale
            p = _softmax_rows(s).astype(BF16)
            o_ref[:, hs] = jnp.dot(p, v_ref[0, :, hs], preferred_element_type=F32).astype(BF16)

    @pl.when(i >= n_prompt)
    def _():
        o_ref[...] = os_ref[...]


def _attn_prompt(q, kt, v, o_sample, nbatch, seq, tq):
    d = q.shape[1]
    mem = v.shape[1]
    hd = d // XA_HEADS
    nt = seq // tq
    n_prompt = nbatch * nt
    n_sample = o_sample.shape[0] // tq
    batch_of = lambda i: jnp.minimum(i // nt, nbatch - 1)
    return pl.pallas_call(
        functools.partial(_attn_prompt_kernel, n_prompt=n_prompt, hd=hd, scale=hd ** -0.5),
        grid=(n_prompt + n_sample,),
        in_specs=[
            pl.BlockSpec((tq, d), lambda i: (jnp.minimum(i, n_prompt - 1), 0)),
            pl.BlockSpec((1, d, mem), lambda i: (batch_of(i), 0, 0)),
            pl.BlockSpec((1, mem, d), lambda i: (batch_of(i), 0, 0)),
            pl.BlockSpec((tq, d), lambda i: (jnp.maximum(i - n_prompt, 0), 0)),
        ],
        out_specs=pl.BlockSpec((tq, d), lambda i: (i, 0)),
        out_shape=jax.ShapeDtypeStruct((nbatch * seq + o_sample.shape[0], d), BF16),
        compiler_params=_params("arbitrary"),
        name="attn_prompt",
    )(q, kt, v, o_sample)


def _attn_sample_kernel(qt_ref, k0_ref, k1_ref, v0_ref, v1_ref, o_ref, *, ts, scale, steps_per_group):
    gb, mem = k0_ref.shape[:2]
    nh, hd, _ = qt_ref.shape[1:]
    sub = pl.program_id(0) % steps_per_group

    @pl.when(sub == 0)
    def _():
        o_ref[...] = jnp.zeros_like(o_ref)

    def head_rows(halves, h):
        flat = [r.reshape(gb, mem * SUBLANES, LANES) for r in halves]
        chunks = hd // LANES
        per_half = chunks // len(halves)
        return jnp.concatenate(
            [jnp.concatenate([flat[c // per_half][j, pl.ds((c % per_half) * nh + h, mem, stride=SUBLANES), :]
                              for c in range(chunks)], axis=1)
             for j in range(gb)], axis=0).astype(BF16)

    lane_owner = lax.broadcasted_iota(jnp.int32, (gb, mem, LANES), 2) // ts
    own = lane_owner == sub * gb + lax.broadcasted_iota(jnp.int32, (gb, mem, LANES), 0)
    for h in range(nh):
        st = jnp.dot(head_rows((k0_ref, k1_ref), h), qt_ref[0, h], preferred_element_type=F32) * scale
        st = st.reshape(gb, mem, LANES)
        e = jnp.exp(st - jnp.max(st, axis=1, keepdims=True))
        p = e / jnp.sum(e, axis=1, keepdims=True)
        pm = jnp.where(own, p, 0.0).astype(BF16).reshape(gb * mem, LANES)
        o = lax.dot_general(pm, head_rows((v0_ref, v1_ref), h), (((0,), (0,)), ((), ())),
                            preferred_element_type=F32)
        o_ref[0, :, h * hd:(h + 1) * hd] += o[0:o_ref.shape[1], :]


def _attn_sample(qt, k_tiles, v_tiles, layer, ts):
    ngroups, nh, hd, _ = qt.shape
    _, nb, mem, halves, _, _ = k_tiles.shape
    assert halves == 2 and nh * (hd // LANES) == halves * SUBLANES
    gb = CACHE_STEP_BATCH
    steps_per_group = nb // ngroups // gb

    def half_spec(half):
        return pl.BlockSpec((None, gb, mem, None, SUBLANES, LANES), lambda i: (layer, i, 0, half, 0, 0))

    return pl.pallas_call(
        functools.partial(_attn_sample_kernel, ts=ts, scale=hd ** -0.5, steps_per_group=steps_per_group),
        grid=(nb // gb,),
        in_specs=[
            pl.BlockSpec((1, nh, hd, LANES), lambda i: (i // steps_per_group, 0, 0, 0)),
            half_spec(0), half_spec(1), half_spec(0), half_spec(1),
        ],
        out_specs=pl.BlockSpec((1, SAMPLE_GROUP * ts, nh * hd), lambda i: (i // steps_per_group, 0, 0)),
        out_shape=jax.ShapeDtypeStruct((ngroups, SAMPLE_GROUP * ts, nh * hd), F32),
        compiler_params=_params("arbitrary"),
        name="attn_sample",
    )(qt, k_tiles, k_tiles, v_tiles, v_tiles)


def _cache_tiles(cache):
    depth, nb, mem, nh, hd = cache.shape
    c = cache.reshape(depth, nb, mem, nh, hd // LANES, LANES)
    return jnp.transpose(c, (0, 1, 2, 4, 3, 5)).reshape(depth, nb, mem, -1, SUBLANES, LANES)


def _pad_to(a, axis, size):
    pad = [(0, 0)] * a.ndim
    pad[axis] = (0, size - a.shape[axis])
    return jnp.pad(a, pad)


def _gate_weights(w_a, b_a, w_i, b_i):
    heads, hd, _ = w_a.shape
    per = LRU_PAIR // hd
    npair = heads // per

    def blockdiag(w):
        w = w.reshape(npair, per, hd, hd)
        eye = jnp.eye(per, dtype=w.dtype)
        return jnp.einsum('pade,ab->padbe', w, eye).reshape(npair, per * hd, per * hd)

    wg = jnp.concatenate([blockdiag(w_a), blockdiag(w_i)], axis=-1).astype(BF16)
    bg = jnp.concatenate([b_a.reshape(npair, 1, per * hd), b_i.reshape(npair, 1, per * hd)], axis=-1)
    return wg, bg


def kernel(x_prompt, x_sample, mem_prompt, cache_mem_k, cache_mem_v, state_conv, state_lru_h, ffn1_norm, ffn1_w_in, ffn1_w_down, mix_norm, w_in, conv_w, conv_b, lru_w_a, lru_b_a, lru_w_i, lru_b_i, lru_lambda, sgu_ln, sgu_w, sgu_b, w_out, xa_norm, xa_w_q, xa_w_kv, xa_w_o, ffn2_norm, ffn2_w_in, ffn2_w_down):
    nbatch, seq, d = x_prompt.shape
    nb, ts, _ = x_sample.shape
    depth = ffn1_norm.shape[0]
    mem = mem_prompt.shape[1]
    dl = conv_w.shape[-1]
    ds = sgu_ln.shape[-1]
    hd = d // XA_HEADS
    d_ff = ffn1_w_down.shape[1]
    mp, ms = nbatch * seq, nb * ts
    assert seq % CHUNK == 0 and ts < CHUNK and ts >= CONV_WIDTH - 1
    assert nb % SAMPLE_GROUP == 0 and SAMPLE_GROUP * ts <= LANES
    assert dl % LRU_PAIR == 0 and ds % SGU_HEADS == 0

    tm = _pick_tile(math.gcd(mp, ms), (512, 256, 128, 64))
    tt = _pick_tile(math.gcd(seq, ms), (256, 128))
    tmem = _pick_tile(nbatch * mem, (512, 256, 128))
    tq = _pick_tile(math.gcd(seq, ms), (512, 256, 128, 64))
    ngroups = nb // SAMPLE_GROUP

    x = jnp.concatenate([x_prompt.reshape(mp, d), jnp.transpose(x_sample, (1, 0, 2)).reshape(ms, d)], axis=0)
    memf = mem_prompt.reshape(nbatch * mem, d)

    gate_up = lambda w: jnp.transpose(w.reshape(depth, d, 2, d_ff), (0, 2, 1, 3)).astype(BF16)
    ffn1_wgu, ffn2_wgu = gate_up(ffn1_w_in), gate_up(ffn2_w_in)
    ffn1_wd, ffn2_wd = ffn1_w_down.astype(BF16), ffn2_w_down.astype(BF16)
    w_in_b, w_out_b = w_in.astype(BF16), w_out.astype(BF16)
    w_q_b, w_kv_b, w_o_b = xa_w_q.astype(BF16), xa_w_kv.astype(BF16), xa_w_o.astype(BF16)
    k_tiles, v_tiles = _cache_tiles(cache_mem_k), _cache_tiles(cache_mem_v)

    outs = {k: [] for k in ("mk", "mv", "cvp", "hp", "cvs", "hs", "vs")}
    for l in range(depth):
        x = _ffn(x, ffn1_norm, ffn1_wgu, ffn1_wd, l, tm)

        z = _norm_matmul(x, mix_norm[l, 0:1], w_in_b, l, tm, _pick_tile(w_in.shape[2], (2048, 1024)), F32, "mix_in")
        wg, bg = _gate_weights(lru_w_a[l], lru_b_a[l], lru_w_i[l], lru_b_i[l])
        cw, cb, lam = conv_w[l], conv_b[l][None, :], lru_lambda[l][None, :]
        sb_rows = jnp.repeat(sgu_b[l].T, ds // SGU_HEADS, axis=1)
        tri = jnp.tril(jnp.ones((ts, ts), F32))
        swx = jnp.repeat(jnp.transpose(sgu_w[l][:, :ts, :ts] * tri, (1, 2, 0)).reshape(ts * ts, SGU_HEADS),
                         ds // SGU_HEADS, axis=1)
        y_s, conv_s, h_s, v_s = _mixer_sample(
            z[mp:], jnp.transpose(state_conv[l], (1, 0, 2)), state_lru_h[l], cw, cb, wg, bg, lam, sgu_ln[l],
            swx, sb_rows[:ts], nb, ts)
        y, xlast, hlast = _mixer_prompt(z, y_s, nbatch, seq, cw, cb, wg, bg, lam, sgu_ln[l], sgu_w[l], sb_rows, tt)
        x = _matmul_norm_res(y, w_out_b, l, x, mix_norm[l, 1:2], tm, "mix_out")
        outs["cvp"].append(xlast[:, SUBLANES - (CONV_WIDTH - 1):, :])
        outs["hp"].append(hlast[:, 0, :])
        outs["cvs"].append(jnp.transpose(conv_s, (1, 0, 2)))
        outs["hs"].append(h_s)
        outs["vs"].append(jnp.transpose(v_s, (1, 0, 2)))

        kv = _norm_matmul(memf, xa_norm[l, 2:3], w_kv_b, l, tmem, 1024, F32, "mem_kv")
        k_mem, v_mem = kv[:, :d], kv[:, d:]
        outs["mk"].append(k_mem.reshape(nbatch, mem, XA_HEADS, hd))
        outs["mv"].append(v_mem.reshape(nbatch, mem, XA_HEADS, hd))
        q = _norm_matmul(x, xa_norm[l, 0:1], w_q_b, l, tm, d, BF16, "xa_q")
        kt = jnp.transpose(k_mem.astype(BF16).reshape(nbatch, mem, d), (0, 2, 1))
        qt = q[mp:].reshape(ts, ngroups, SAMPLE_GROUP, XA_HEADS, hd)
        qt = jnp.transpose(qt, (1, 3, 4, 2, 0)).reshape(ngroups, XA_HEADS, hd, SAMPLE_GROUP * ts)
        qt = _pad_to(qt, 3, LANES)
        o_s = _attn_sample(qt, k_tiles, v_tiles, l, ts).astype(BF16)
        o_s = jnp.transpose(o_s.reshape(ngroups, SAMPLE_GROUP, ts, d), (2, 0, 1, 3)).reshape(ms, d)
        o = _attn_prompt(q, kt, v_mem.astype(BF16).reshape(nbatch, mem, d), o_s, nbatch, seq, tq)
        x = _matmul_norm_res(o, w_o_b, l, x, xa_norm[l, 1:2], tm, "xa_out")

        x = _ffn(x, ffn2_norm, ffn2_wgu, ffn2_wd, l, tm)

    y_prompt = x[:mp].reshape(nbatch, seq, d)
    y_sample = jnp.transpose(x[mp:].reshape(ts, nb, d), (1, 0, 2))
    return (y_prompt, y_sample, jnp.stack(outs["mk"]), jnp.stack(outs["mv"]), jnp.stack(outs["cvp"]),
            jnp.stack(outs["hp"]), jnp.stack(outs["cvs"]), jnp.stack(outs["hs"]), jnp.stack(outs["vs"]))
```

```python
import functools
import math

import jax
import jax.numpy as jnp
from jax import lax
from jax.experimental import pallas as pl
from jax.experimental.pallas import tpu as pltpu

F32 = jnp.float32
BF16 = jnp.bfloat16

EPS = 1e-6
HALF = 0.5
LRU_C = 8.0
CONV_WIDTH = 4
CHUNK = 128
LRU_HEADS = 8
SGU_HEADS = 8
XA_HEADS = 4

VMEM_LIMIT_BYTES = 56 * 1024 * 1024
SUBLANES = 8
LANES = 128
FF_TILE = 1024
SAMPLE_GROUP = 8
CACHE_STEP_BATCH = 4
LRU_PAIR = 2 * LANES


def _params(*sem):
    return pltpu.CompilerParams(dimension_semantics=sem, vmem_limit_bytes=VMEM_LIMIT_BYTES)


def _pick_tile(n, prefs):
    for p in prefs:
        if n % p == 0:
            return p
    raise ValueError(f"no tile in {prefs} divides {n}")


def _rms(x, g):
    return (x * lax.rsqrt(jnp.mean(x * x, axis=-1, keepdims=True) + EPS)) * g


def _gelu(x):
    c = math.sqrt(2.0 / math.pi)
    return 0.5 * x * (1.0 + jnp.tanh(c * (x + 0.044715 * (x * x * x))))


def _softplus(x):
    return jnp.maximum(x, 0.0) + jnp.log1p(jnp.exp(-jnp.abs(x)))


def _ffn_kernel(x_ref, g_ref, wg_ref, wu_ref, wd_ref, o_ref, h_scr, *, nf, last_cols):
    f = pl.program_id(1)

    @pl.when(f == 0)
    def _():
        h_scr[...] = _rms(x_ref[...], g_ref[0:1, :]).astype(BF16)
        o_ref[...] = jnp.zeros_like(o_ref)

    def accumulate(cols):
        h = h_scr[...]
        gate = jnp.dot(h, wg_ref[:, :cols], preferred_element_type=F32)
        up = jnp.dot(h, wu_ref[0, :, FF_TILE - cols:], preferred_element_type=F32)
        act = ((gate * jax.nn.sigmoid(gate)) * up).astype(BF16)
        o_ref[...] += jnp.dot(act, wd_ref[:cols, :], preferred_element_type=F32)

    if last_cols == FF_TILE:
        accumulate(FF_TILE)
    else:
        @pl.when(f < nf - 1)
        def _():
            accumulate(FF_TILE)

        @pl.when(f == nf - 1)
        def _():
            accumulate(last_cols)

    @pl.when(f == nf - 1)
    def _():
        o_ref[...] = x_ref[...] + HALF * _rms(o_ref[...], g_ref[1:2, :])


def _ffn(x, norm, w_in, wd, layer, tm):
    m, d = x.shape
    ff = wd.shape[1]
    assert ff % LANES == 0 and FF_TILE % LANES == 0 and ff >= FF_TILE
    nf = pl.cdiv(ff, FF_TILE)
    up_start = lambda f: jnp.minimum(ff // LANES + f * (FF_TILE // LANES), (2 * ff - FF_TILE) // LANES) * LANES
    return pl.pallas_call(
        functools.partial(_ffn_kernel, nf=nf, last_cols=ff - (nf - 1) * FF_TILE),
        grid=(m // tm, nf),
        in_specs=[
            pl.BlockSpec((tm, d), lambda i, f: (i, 0)),
            pl.BlockSpec((None, 2, d), lambda i, f: (layer, 0, 0)),
            pl.BlockSpec((None, d, FF_TILE), lambda i, f: (layer, 0, f)),
            pl.BlockSpec((pl.Element(1), pl.Element(d), pl.Element(FF_TILE)), lambda i, f: (layer, 0, up_start(f))),
            pl.BlockSpec((None, FF_TILE, d), lambda i, f: (layer, f, 0)),
        ],
        out_specs=pl.BlockSpec((tm, d), lambda i, f: (i, 0)),
        out_shape=jax.ShapeDtypeStruct((m, d), F32),
        scratch_shapes=[pltpu.VMEM((tm, d), BF16)],
        compiler_params=_params("parallel", "arbitrary"),
        name="ffn",
    )(x, norm, w_in, w_in, wd)


def _norm_matmul_kernel(x_ref, g_ref, w_ref, o_ref, h_scr):
    @pl.when(pl.program_id(1) == 0)
    def _():
        h_scr[...] = _rms(x_ref[...], g_ref[...]).astype(BF16)

    o_ref[...] = jnp.dot(h_scr[...], w_ref[...], preferred_element_type=F32).astype(o_ref.dtype)


def _norm_matmul(x, g, w, layer, tm, tn, out_dtype, name, row_block0=0, rows=None):
    d = x.shape[1]
    m = x.shape[0] if rows is None else rows
    n = w.shape[2]
    return pl.pallas_call(
        _norm_matmul_kernel,
        grid=(m // tm, n // tn),
        in_specs=[
            pl.BlockSpec((tm, d), lambda i, j: (i + row_block0, 0)),
            pl.BlockSpec((1, d), lambda i, j: (0, 0)),
            pl.BlockSpec((None, d, tn), lambda i, j: (layer, 0, j)),
        ],
        out_specs=pl.BlockSpec((tm, tn), lambda i, j: (i, j)),
        out_shape=jax.ShapeDtypeStruct((m, n), out_dtype),
        scratch_shapes=[pltpu.VMEM((tm, d), BF16)],
        compiler_params=_params("parallel", "arbitrary"),
        name=name,
    )(x, g, w)


def _matmul_norm_res_kernel(y_ref, w_ref, x_ref, g_ref, o_ref):
    d = jnp.dot(y_ref[...], w_ref[...], preferred_element_type=F32)
    o_ref[...] = x_ref[...] + _rms(d, g_ref[...])


def _matmul_norm_res(y, w, layer, x, g, tm, name):
    m, k = y.shape
    n = w.shape[2]
    return pl.pallas_call(
        _matmul_norm_res_kernel,
        grid=(m // tm,),
        in_specs=[
            pl.BlockSpec((tm, k), lambda i: (i, 0)),
            pl.BlockSpec((None, k, n), lambda i: (layer, 0, 0)),
            pl.BlockSpec((tm, n), lambda i: (i, 0)),
            pl.BlockSpec((1, n), lambda i: (0, 0)),
        ],
        out_specs=pl.BlockSpec((tm, n), lambda i: (i, 0)),
        out_shape=jax.ShapeDtypeStruct((m, n), F32),
        compiler_params=_params("parallel"),
        name=name,
    )(y, w, x, g)


def _lru_gates(xc, wg, bg, c):
    g = jnp.dot(xc.astype(BF16), wg, preferred_element_type=F32) + bg
    r = jax.nn.sigmoid(g[:, :LRU_PAIR])
    i = jax.nn.sigmoid(g[:, LRU_PAIR:])
    log_a = c * r
    a = jnp.exp(log_a)
    b = jnp.sqrt(-jnp.tanh(log_a) * (a * a + 1.0)) * (i * xc)
    return a, b


def _layer_norm(x, g, b):
    mu = jnp.mean(x, axis=-1, keepdims=True)
    xc = x - mu
    return (xc * lax.rsqrt(jnp.mean(xc * xc, axis=-1, keepdims=True) + EPS)) * g + b


def _mix_prompt_kernel(x_ref, g_ref, w_ref, cw_ref, cb_ref, wg_ref, bg_ref, lam_ref, ln_ref, sw_ref, sb_ref, ys_ref,
                       y_ref, xlast_ref, hlast_ref, z_a, z_b, xprev_scr, hc_scr, xc_scr,
                       *, nt, n_prompt, tt, dl, ds):
    i = pl.program_id(0)
    t = (i + nt - 1) % nt

    @pl.when(i == 0)
    def _():
        z_b[...] = jnp.zeros_like(z_b)

    @pl.when((t == 0) | (i == 0))
    def _():
        xprev_scr[...] = jnp.zeros_like(xprev_scr)
        hc_scr[...] = jnp.zeros_like(hc_scr)

    def step(z_write, z_read):
        h = _rms(x_ref[...], g_ref[...]).astype(BF16)
        dz = w_ref.shape[1]

        def project(k, of):
            cols = slice(k * dz // of, (k + 1) * dz // of)
            z_write[:, cols] = jnp.dot(h, w_ref[:, cols], preferred_element_type=F32)

        _mixer_prompt_tile(z_read, cw_ref, cb_ref, wg_ref, bg_ref, lam_ref, ln_ref, sw_ref, sb_ref,
                           y_ref, xlast_ref, hlast_ref, xprev_scr, hc_scr, xc_scr, project, tt=tt, dl=dl, ds=ds)

    @pl.when((i <= n_prompt) & (i % 2 == 0))
    def _():
        step(z_a, z_b)

    @pl.when((i <= n_prompt) & (i % 2 == 1))
    def _():
        step(z_b, z_a)

    @pl.when(i > n_prompt)
    def _():
        y_ref[...] = ys_ref[...]


def _mixer_prompt_tile(z_ref, cw_ref, cb_ref, wg_ref, bg_ref, lam_ref, ln_ref, sw_ref, sb_ref,
                       y_ref, xlast_ref, hlast_ref, xprev_scr, hc_scr, xc_scr, project, *, tt, dl, ds):
    ng = tt // SUBLANES
    npairs = dl // LRU_PAIR
    nslices = 2 * npairs
    project(0, nslices)

    x = z_ref[:, 0:dl]
    w = cw_ref[...]
    cb = cb_ref[...]
    acc = cb + w[CONV_WIDTH - 1:CONV_WIDTH, :] * x
    for k in range(1, CONV_WIDTH):
        acc = acc + w[CONV_WIDTH - 1 - k:CONV_WIDTH - k, :] * pltpu.roll(x, k, 0)
    xc_scr[...] = acc
    x_head = x[0:SUBLANES, :]
    ext = jnp.concatenate([xprev_scr[...], x_head], axis=0)
    acc = cb + w[CONV_WIDTH - 1:CONV_WIDTH, :] * x_head
    for k in range(1, CONV_WIDTH):
        acc = acc + w[CONV_WIDTH - 1 - k:CONV_WIDTH - k, :] * pltpu.roll(ext, k, 0)[SUBLANES:, :]
    xc_scr[0:SUBLANES, :] = acc
    x_tail = x[tt - SUBLANES:tt, :]
    xprev_scr[...] = x_tail
    xlast_ref[0] = x_tail

    c_all = -LRU_C * _softplus(-lam_ref[...])
    sidx = lax.broadcasted_iota(jnp.int32, (ng, SUBLANES, LRU_PAIR), 1)
    for p in range(npairs):
        sl = slice(p * LRU_PAIR, (p + 1) * LRU_PAIR)
        project(1 + p, nslices)
        a, b = _lru_gates(xc_scr[:, sl], wg_ref[p], bg_ref[p], c_all[:, sl])
        a = a.reshape(ng, SUBLANES, LRU_PAIR)
        b = b.reshape(ng, SUBLANES, LRU_PAIR)
        shift = 1
        while shift < SUBLANES:
            keep = sidx >= shift
            b = jnp.where(keep, a * pltpu.roll(b, shift, 1) + b, b)
            a = jnp.where(keep, a * pltpu.roll(a, shift, 1), a)
            shift *= 2
        carry = hc_scr[:, sl]
        groups = []
        for gi in range(ng):
            hg = a[gi] * carry + b[gi]
            groups.append(hg)
            carry = hg[SUBLANES - 1:SUBLANES, :]
        hc_scr[:, sl] = carry
        hlast_ref[0, :, sl] = carry
        h_seq = jnp.concatenate(groups, axis=0)
        y_ref[:, sl] = (h_seq * _gelu(z_ref[:, dl + p * LRU_PAIR:dl + (p + 1) * LRU_PAIR])).astype(BF16)

    project(npairs + 1, nslices)
    v = _layer_norm(_gelu(z_ref[:, 2 * dl + ds:2 * dl + 2 * ds]), ln_ref[0:1, :], ln_ref[1:2, :])
    vb = v.astype(BF16)
    project(npairs + 2, nslices)
    u = _gelu(z_ref[:, 2 * dl:2 * dl + ds])
    nc = tt // CHUNK
    hd = ds // SGU_HEADS
    row = lax.broadcasted_iota(jnp.int32, (CHUNK, CHUNK), 0)
    col = lax.broadcasted_iota(jnp.int32, (CHUNK, CHUNK), 1)
    for g in range(SGU_HEADS):
        if g == SGU_HEADS // 2:
            project(npairs + 3, nslices)
        cs = slice(g * hd, (g + 1) * hd)
        wt = jnp.where(row >= col, sw_ref[g], 0.0).astype(BF16)
        vg = jnp.concatenate([vb[c * CHUNK:(c + 1) * CHUNK, cs] for c in range(nc)], axis=1)
        s = jnp.dot(wt, vg, preferred_element_type=F32)
        for c in range(nc):
            rs = slice(c * CHUNK, (c + 1) * CHUNK)
            sc = s[:, c * hd:(c + 1) * hd] + sb_ref[:, cs]
            y_ref[rs, dl + g * hd:dl + (g + 1) * hd] = (u[rs, cs] * sc).astype(BF16)
    assert npairs + 4 == nslices


def _mix_prompt(x, g, w_in, layer, ys, nbatch, seq, cw, cb, wg, bg, lam, ln, sw, sb, tt):
    d = x.shape[1]
    dz = w_in.shape[2]
    dl = cw.shape[1]
    ds = ln.shape[1]
    nt = seq // tt
    n_prompt = nbatch * nt
    n_sample = ys.shape[0] // tt
    full = lambda *shape: pl.BlockSpec(shape, lambda i: (0,) * len(shape))
    batch_of = lambda i: jnp.clip((i - 1) // nt, 0, nbatch - 1)
    return pl.pallas_call(
        functools.partial(_mix_prompt_kernel, nt=nt, n_prompt=n_prompt, tt=tt, dl=dl, ds=ds),
        grid=(n_prompt + 1 + n_sample,),
        in_specs=[
            pl.BlockSpec((tt, d), lambda i: (jnp.minimum(i, n_prompt - 1), 0)),
            full(*g.shape),
            pl.BlockSpec((None, d, dz), lambda i: (layer, 0, 0), pipeline_mode=pl.Buffered(1)),
            full(*cw.shape), full(*cb.shape), full(*wg.shape), full(*bg.shape), full(*lam.shape),
            full(*ln.shape), full(*sw.shape), full(*sb.shape),
            pl.BlockSpec((tt, dl + ds), lambda i: (jnp.clip(i - n_prompt - 1, 0, n_sample - 1), 0)),
        ],
        out_specs=[
            pl.BlockSpec((tt, dl + ds), lambda i: (jnp.maximum(i - 1, 0), 0)),
            pl.BlockSpec((1, SUBLANES, dl), lambda i: (batch_of(i), 0, 0)),
            pl.BlockSpec((1, 1, dl), lambda i: (batch_of(i), 0, 0)),
        ],
        out_shape=[
            jax.ShapeDtypeStruct((nbatch * seq + ys.shape[0], dl + ds), BF16),
            jax.ShapeDtypeStruct((nbatch, SUBLANES, dl), F32),
            jax.ShapeDtypeStruct((nbatch, 1, dl), F32),
        ],
        scratch_shapes=[
            pltpu.VMEM((tt, dz), F32),
            pltpu.VMEM((tt, dz), F32),
            pltpu.VMEM((SUBLANES, dl), F32),
            pltpu.VMEM((1, dl), F32),
            pltpu.VMEM((tt, dl), F32),
        ],
        compiler_params=_params("arbitrary"),
        name="mix_prompt",
    )(x, g, w_in, cw, cb, wg, bg, lam, ln, sw, sb, ys)


def _mixer_sample_kernel(z_ref, cbuf_ref, h0_ref, cw_ref, cb_ref, wg_ref, bg_ref, lam_ref, ln_ref,
                         swx_ref, sbx_ref,
                         y_ref, convnew_ref, hlast_ref, v_ref, xc_scr, *, nb, ts, dl, ds):
    rows = lambda t: slice(t * nb, (t + 1) * nb)
    hist = CONV_WIDTH - 1
    xx = [cbuf_ref[k] for k in range(hist)] + [z_ref[rows(t), 0:dl] for t in range(ts)]
    w = cw_ref[...]
    cb = cb_ref[...]
    for t in range(ts):
        acc = cb + w[0:1, :] * xx[t]
        for k in range(1, CONV_WIDTH):
            acc = acc + w[k:k + 1, :] * xx[t + k]
        xc_scr[rows(t), :] = acc
    for k in range(hist):
        convnew_ref[k] = xx[ts + k]

    c_all = -LRU_C * _softplus(-lam_ref[...])
    for p in range(dl // LRU_PAIR):
        sl = slice(p * LRU_PAIR, (p + 1) * LRU_PAIR)
        a, b = _lru_gates(xc_scr[:, sl], wg_ref[p], bg_ref[p], c_all[:, sl])
        h = h0_ref[:, sl]
        for t in range(ts):
            h = a[rows(t), :] * h + b[rows(t), :]
            gl = z_ref[rows(t), dl + p * LRU_PAIR:dl + (p + 1) * LRU_PAIR]
            y_ref[rows(t), sl] = (h * _gelu(gl)).astype(BF16)
        hlast_ref[:, sl] = h

    vs = []
    for t in range(ts):
        v = _layer_norm(_gelu(z_ref[rows(t), 2 * dl + ds:2 * dl + 2 * ds]), ln_ref[0:1, :], ln_ref[1:2, :])
        v_ref[t] = v
        vs.append(v)
    for t in range(ts):
        s = sbx_ref[t:t + 1, :] + swx_ref[t * ts:t * ts + 1, :] * vs[0]
        for j in range(1, t + 1):
            s = s + swx_ref[t * ts + j:t * ts + j + 1, :] * vs[j]
        u = _gelu(z_ref[rows(t), 2 * dl:2 * dl + ds])
        y_ref[rows(t), dl:dl + ds] = (u * s).astype(BF16)


def _mixer_sample(z, cbuf, h0, cw, cb, wg, bg, lam, ln, swx, sbx, nb, ts):
    dl = cw.shape[1]
    ds = ln.shape[1]
    return pl.pallas_call(
        functools.partial(_mixer_sample_kernel, nb=nb, ts=ts, dl=dl, ds=ds),
        out_shape=[
            jax.ShapeDtypeStruct((ts * nb, dl + ds), BF16),
            jax.ShapeDtypeStruct((CONV_WIDTH - 1, nb, dl), F32),
            jax.ShapeDtypeStruct((nb, dl), F32),
            jax.ShapeDtypeStruct((ts, nb, ds), F32),
        ],
        scratch_shapes=[pltpu.VMEM((ts * nb, dl), F32)],
        compiler_params=pltpu.CompilerParams(vmem_limit_bytes=VMEM_LIMIT_BYTES),
        name="mixer_sample",
    )(z, cbuf, h0, cw, cb, wg, bg, lam, ln, swx, sbx)


def _softmax_rows(s):
    e = jnp.exp(s - jnp.max(s, axis=-1, keepdims=True))
    return e / jnp.sum(e, axis=-1, keepdims=True)


def _attn_prompt_kernel(q_ref, kt_ref, v_ref, os_ref, o_ref, *, n_prompt, hd, scale):
    i = pl.program_id(0)

    @pl.when(i < n_prompt)
    def _():
        for h in range(XA_HEADS):
            hs = slice(h * hd, (h + 1) * hd)
            s = jnp.dot(q_ref[:, hs], kt_ref[0, hs, :], preferred_element_type=F32) * scale
            p = _softmax_rows(s).astype(BF16)
            o_ref[:, hs] = jnp.dot(p, v_ref[0, :, hs], preferred_element_type=F32).astype(BF16)

    @pl.when(i >= n_prompt)
    def _():
        o_ref[...] = os_ref[...]


def _attn_prompt(q, kt, v, o_sample, nbatch, seq, tq):
    d = q.shape[1]
    mem = v.shape[1]
    hd = d // XA_HEADS
    nt = seq // tq
    n_prompt = nbatch * nt
    n_sample = o_sample.shape[0] // tq
    batch_of = lambda i: jnp.minimum(i // nt, nbatch - 1)
    return pl.pallas_call(
        functools.partial(_attn_prompt_kernel, n_prompt=n_prompt, hd=hd, scale=hd ** -0.5),
        grid=(n_prompt + n_sample,),
        in_specs=[
            pl.BlockSpec((tq, d), lambda i: (jnp.minimum(i, n_prompt - 1), 0)),
            pl.BlockSpec((1, d, mem), lambda i: (batch_of(i), 0, 0)),
            pl.BlockSpec((1, mem, d), lambda i: (batch_of(i), 0, 0)),
            pl.BlockSpec((tq, d), lambda i: (jnp.maximum(i - n_prompt, 0), 0)),
        ],
        out_specs=pl.BlockSpec((tq, d), lambda i: (i, 0)),
        out_shape=jax.ShapeDtypeStruct((nbatch * seq + o_sample.shape[0], d), BF16),
        compiler_params=_params("arbitrary"),
        name="attn_prompt",
    )(q, kt, v, o_sample)


def _attn_sample_kernel(qt_ref, k0_ref, k1_ref, v0_ref, v1_ref, o_ref, *, ts, scale, steps_per_group):
    gb, mem = k0_ref.shape[:2]
    nh, hd, _ = qt_ref.shape[1:]
    sub = pl.program_id(0) % steps_per_group

    @pl.when(sub == 0)
    def _():
        o_ref[...] = jnp.zeros_like(o_ref)

    def head_rows(halves, h):
        flat = [r.reshape(gb, mem * SUBLANES, LANES) for r in halves]
        chunks = hd // LANES
        per_half = chunks // len(halves)
        return jnp.concatenate(
            [jnp.concatenate([flat[c // per_half][j, pl.ds((c % per_half) * nh + h, mem, stride=SUBLANES), :]
                              for c in range(chunks)], axis=1)
             for j in range(gb)], axis=0).astype(BF16)

    lane_owner = lax.broadcasted_iota(jnp.int32, (gb, mem, LANES), 2) // ts
    own = lane_owner == sub * gb + lax.broadcasted_iota(jnp.int32, (gb, mem, LANES), 0)
    for h in range(nh):
        st = jnp.dot(head_rows((k0_ref, k1_ref), h), qt_ref[0, h], preferred_element_type=F32) * scale
        st = st.reshape(gb, mem, LANES)
        e = jnp.exp(st - jnp.max(st, axis=1, keepdims=True))
        p = e / jnp.sum(e, axis=1, keepdims=True)
        pm = jnp.where(own, p, 0.0).astype(BF16).reshape(gb * mem, LANES)
        o = lax.dot_general(pm, head_rows((v0_ref, v1_ref), h), (((0,), (0,)), ((), ())),
                            preferred_element_type=F32)
        o_ref[0, :, h * hd:(h + 1) * hd] += o[0:o_ref.shape[1], :]


def _attn_sample(qt, k_tiles, v_tiles, layer, ts):
    ngroups, nh, hd, _ = qt.shape
    _, nb, mem, halves, _, _ = k_tiles.shape
    assert halves == 2 and nh * (hd // LANES) == halves * SUBLANES
    gb = CACHE_STEP_BATCH
    steps_per_group = nb // ngroups // gb

    def half_spec(half):
        return pl.BlockSpec((None, gb, mem, None, SUBLANES, LANES), lambda i: (layer, i, 0, half, 0, 0))

    return pl.pallas_call(
        functools.partial(_attn_sample_kernel, ts=ts, scale=hd ** -0.5, steps_per_group=steps_per_group),
        grid=(nb // gb,),
        in_specs=[
            pl.BlockSpec((1, nh, hd, LANES), lambda i: (i // steps_per_group, 0, 0, 0)),
            half_spec(0), half_spec(1), half_spec(0), half_spec(1),
        ],
        out_specs=pl.BlockSpec((1, SAMPLE_GROUP * ts, nh * hd), lambda i: (i // steps_per_group, 0, 0)),
        out_shape=jax.ShapeDtypeStruct((ngroups, SAMPLE_GROUP * ts, nh * hd), F32),
        compiler_params=_params("arbitrary"),
        name="attn_sample",
    )(qt, k_tiles, k_tiles, v_tiles, v_tiles)


def _cache_tiles(cache):
    depth, nb, mem, nh, hd = cache.shape
    c = cache.reshape(depth, nb, mem, nh, hd // LANES, LANES)
    return jnp.transpose(c, (0, 1, 2, 4, 3, 5)).reshape(depth, nb, mem, -1, SUBLANES, LANES)


def _pad_to(a, axis, size):
    pad = [(0, 0)] * a.ndim
    pad[axis] = (0, size - a.shape[axis])
    return jnp.pad(a, pad)


def _gate_weights(w_a, b_a, w_i, b_i):
    heads, hd, _ = w_a.shape
    per = LRU_PAIR // hd
    npair = heads // per

    def blockdiag(w):
        w = w.reshape(npair, per, hd, hd)
        eye = jnp.eye(per, dtype=w.dtype)
        return jnp.einsum('pade,ab->padbe', w, eye).reshape(npair, per * hd, per * hd)

    wg = jnp.concatenate([blockdiag(w_a), blockdiag(w_i)], axis=-1).astype(BF16)
    bg = jnp.concatenate([b_a.reshape(npair, 1, per * hd), b_i.reshape(npair, 1, per * hd)], axis=-1)
    return wg, bg


def kernel(x_prompt, x_sample, mem_prompt, cache_mem_k, cache_mem_v, state_conv, state_lru_h, ffn1_norm, ffn1_w_in, ffn1_w_down, mix_norm, w_in, conv_w, conv_b, lru_w_a, lru_b_a, lru_w_i, lru_b_i, lru_lambda, sgu_ln, sgu_w, sgu_b, w_out, xa_norm, xa_w_q, xa_w_kv, xa_w_o, ffn2_norm, ffn2_w_in, ffn2_w_down):
    nbatch, seq, d = x_prompt.shape
    nb, ts, _ = x_sample.shape
    depth = ffn1_norm.shape[0]
    mem = mem_prompt.shape[1]
    dl = conv_w.shape[-1]
    ds = sgu_ln.shape[-1]
    hd = d // XA_HEADS
    d_ff = ffn1_w_down.shape[1]
    mp, ms = nbatch * seq, nb * ts
    assert seq % CHUNK == 0 and ts < CHUNK and ts >= CONV_WIDTH - 1
    assert nb % SAMPLE_GROUP == 0 and SAMPLE_GROUP * ts <= LANES
    assert dl % LRU_PAIR == 0 and ds % SGU_HEADS == 0

    tm = _pick_tile(math.gcd(mp, ms), (512, 256, 128, 64))
    tt = _pick_tile(math.gcd(seq, ms), (256, 128))
    tmem = _pick_tile(nbatch * mem, (512, 256, 128))
    tq = _pick_tile(math.gcd(seq, ms), (512, 256, 128, 64))
    ngroups = nb // SAMPLE_GROUP

    x = jnp.concatenate([x_prompt.reshape(mp, d), jnp.transpose(x_sample, (1, 0, 2)).reshape(ms, d)], axis=0)
    memf = mem_prompt.reshape(nbatch * mem, d)

    ffn1_wi, ffn2_wi = ffn1_w_in.astype(BF16), ffn2_w_in.astype(BF16)
    ffn1_wd, ffn2_wd = ffn1_w_down.astype(BF16), ffn2_w_down.astype(BF16)
    w_in_b, w_out_b = w_in.astype(BF16), w_out.astype(BF16)
    w_q_b, w_kv_b, w_o_b = xa_w_q.astype(BF16), xa_w_kv.astype(BF16), xa_w_o.astype(BF16)
    k_tiles, v_tiles = _cache_tiles(cache_mem_k), _cache_tiles(cache_mem_v)

    outs = {k: [] for k in ("mk", "mv", "cvp", "hp", "cvs", "hs", "vs")}
    for l in range(depth):
        x = _ffn(x, ffn1_norm, ffn1_wi, ffn1_wd, l, tm)

        wg, bg = _gate_weights(lru_w_a[l], lru_b_a[l], lru_w_i[l], lru_b_i[l])
        cw, cb, lam = conv_w[l], conv_b[l][None, :], lru_lambda[l][None, :]
        sb_rows = jnp.repeat(sgu_b[l].T, ds // SGU_HEADS, axis=1)
        tri = jnp.tril(jnp.ones((ts, ts), F32))
        swx = jnp.repeat(jnp.transpose(sgu_w[l][:, :ts, :ts] * tri, (1, 2, 0)).reshape(ts * ts, SGU_HEADS),
                         ds // SGU_HEADS, axis=1)
        z_s = _norm_matmul(x, mix_norm[l, 0:1], w_in_b, l, tm, _pick_tile(w_in.shape[2], (2048, 1024)), F32,
                           "mix_in_sample", row_block0=mp // tm, rows=ms)
        y_s, conv_s, h_s, v_s = _mixer_sample(
            z_s, jnp.transpose(state_conv[l], (1, 0, 2)), state_lru_h[l], cw, cb, wg, bg, lam, sgu_ln[l],
            swx, sb_rows[:ts], nb, ts)
        y, xlast, hlast = _mix_prompt(x, mix_norm[l, 0:1], w_in_b, l, y_s, nbatch, seq, cw, cb, wg, bg, lam,
                                      sgu_ln[l], sgu_w[l], sb_rows, tt)
        x = _matmul_norm_res(y, w_out_b, l, x, mix_norm[l, 1:2], tm, "mix_out")
        outs["cvp"].append(xlast[:, SUBLANES - (CONV_WIDTH - 1):, :])
        outs["hp"].append(hlast[:, 0, :])
        outs["cvs"].append(jnp.transpose(conv_s, (1, 0, 2)))
        outs["hs"].append(h_s)
        outs["vs"].append(jnp.transpose(v_s, (1, 0, 2)))

        kv = _norm_matmul(memf, xa_norm[l, 2:3], w_kv_b, l, tmem, 1024, F32, "mem_kv")
        k_mem, v_mem = kv[:, :d], kv[:, d:]
        outs["mk"].append(k_mem.reshape(nbatch, mem, XA_HEADS, hd))
        outs["mv"].append(v_mem.reshape(nbatch, mem, XA_HEADS, hd))
        q = _norm_matmul(x, xa_norm[l, 0:1], w_q_b, l, tm, d, BF16, "xa_q")
        kt = jnp.transpose(k_mem.astype(BF16).reshape(nbatch, mem, d), (0, 2, 1))
        qt = q[mp:].reshape(ts, ngroups, SAMPLE_GROUP, XA_HEADS, hd)
        qt = jnp.transpose(qt, (1, 3, 4, 2, 0)).reshape(ngroups, XA_HEADS, hd, SAMPLE_GROUP * ts)
        qt = _pad_to(qt, 3, LANES)
        o_s = _attn_sample(qt, k_tiles, v_tiles, l, ts).astype(BF16)
        o_s = jnp.transpose(o_s.reshape(ngroups, SAMPLE_GROUP, ts, d), (2, 0, 1, 3)).reshape(ms, d)
        o = _attn_prompt(q, kt, v_mem.astype(BF16).reshape(nbatch, mem, d), o_s, nbatch, seq, tq)
        x = _matmul_norm_res(o, w_o_b, l, x, xa_norm[l, 1:2], tm, "xa_out")

        x = _ffn(x, ffn2_norm, ffn2_wi, ffn2_wd, l, tm)

    y_prompt = x[:mp].reshape(nbatch, seq, d)
    y_sample = jnp.transpose(x[mp:].reshape(ts, nb, d), (1, 0, 2))
    return (y_prompt, y_sample, jnp.stack(outs["mk"]), jnp.stack(outs["mv"]), jnp.stack(outs["cvp"]),
            jnp.stack(outs["hp"]), jnp.stack(outs["cvs"]), jnp.stack(outs["hs"]), jnp.stack(outs["vs"]))
```

```python
import functools
import math

import jax
import jax.numpy as jnp
from jax import lax
from jax.experimental import pallas as pl
from jax.experimental.pallas import tpu as pltpu

F32 = jnp.float32
BF16 = jnp.bfloat16

EPS = 1e-6
HALF = 0.5
LRU_C = 8.0
CONV_WIDTH = 4
CHUNK = 128
LRU_HEADS = 8
SGU_HEADS = 8
XA_HEADS = 4

VMEM_LIMIT_BYTES = 56 * 1024 * 1024
SUBLANES = 8
LANES = 128
FF_TILE = 1024
SAMPLE_GROUP = 8
CACHE_STEP_BATCH = 4
LRU_PAIR = 2 * LANES


def _params(*sem):
    return pltpu.CompilerParams(dimension_semantics=sem, vmem_limit_bytes=VMEM_LIMIT_BYTES)


def _pick_tile(n, prefs):
    for p in prefs:
        if n % p == 0:
            return p
    raise ValueError(f"no tile in {prefs} divides {n}")


def _rms(x, g):
    return (x * lax.rsqrt(jnp.mean(x * x, axis=-1, keepdims=True) + EPS)) * g


def _gelu(x):
    c = math.sqrt(2.0 / math.pi)
    return 0.5 * x * (1.0 + jnp.tanh(c * (x + 0.044715 * (x * x * x))))


def _softplus(x):
    return jnp.maximum(x, 0.0) + jnp.log1p(jnp.exp(-jnp.abs(x)))


def _ffn_kernel(x_ref, *refs, nf, last_cols, n_first, two_sources):
    i, f = pl.program_id(0), pl.program_id(1)
    if two_sources:
        xs_ref, g_ref, wg_ref, wu_ref, wd_ref, o_ref, h_scr = refs
        read_x = lambda: jnp.where(i < n_first, x_ref[...], xs_ref[...])
    else:
        g_ref, wg_ref, wu_ref, wd_ref, o_ref, h_scr = refs
        read_x = lambda: x_ref[...]

    @pl.when(f == 0)
    def _():
        h_scr[...] = _rms(read_x(), g_ref[0:1, :]).astype(BF16)
        o_ref[...] = jnp.zeros_like(o_ref)

    def accumulate(cols):
        h = h_scr[...]
        gate = jnp.dot(h, wg_ref[:, :cols], preferred_element_type=F32)
        up = jnp.dot(h, wu_ref[0, :, FF_TILE - cols:], preferred_element_type=F32)
        act = ((gate * jax.nn.sigmoid(gate)) * up).astype(BF16)
        o_ref[...] += jnp.dot(act, wd_ref[:cols, :], preferred_element_type=F32)

    if last_cols == FF_TILE:
        accumulate(FF_TILE)
    else:
        @pl.when(f < nf - 1)
        def _():
            accumulate(FF_TILE)

        @pl.when(f == nf - 1)
        def _():
            accumulate(last_cols)

    @pl.when(f == nf - 1)
    def _():
        o_ref[...] = read_x() + HALF * _rms(o_ref[...], g_ref[1:2, :])


def _ffn(x, norm, w_in, wd, layer, tm, xs=None, row_block0=0, rows=None):
    d = x.shape[1]
    n_first = (x.shape[0] if rows is None else rows) // tm
    n_second = 0 if xs is None else xs.shape[0] // tm
    second = [] if xs is None else [(xs, pl.BlockSpec((tm, d), lambda i, f: (jnp.maximum(i - n_first, 0), 0)))]
    ff = wd.shape[1]
    assert ff % LANES == 0 and FF_TILE % LANES == 0 and ff >= FF_TILE
    nf = pl.cdiv(ff, FF_TILE)
    up_start = lambda f: jnp.minimum(ff // LANES + f * (FF_TILE // LANES), (2 * ff - FF_TILE) // LANES) * LANES
    return pl.pallas_call(
        functools.partial(_ffn_kernel, nf=nf, last_cols=ff - (nf - 1) * FF_TILE, n_first=n_first,
                          two_sources=xs is not None),
        grid=(n_first + n_second, nf),
        in_specs=[
            pl.BlockSpec((tm, d), lambda i, f: (jnp.minimum(i, n_first - 1) + row_block0, 0)),
            *[spec for _, spec in second],
            pl.BlockSpec((None, 2, d), lambda i, f: (layer, 0, 0)),
            pl.BlockSpec((None, d, FF_TILE), lambda i, f: (layer, 0, f)),
            pl.BlockSpec((pl.Element(1), pl.Element(d), pl.Element(FF_TILE)), lambda i, f: (layer, 0, up_start(f))),
            pl.BlockSpec((None, FF_TILE, d), lambda i, f: (layer, f, 0)),
        ],
        out_specs=pl.BlockSpec((tm, d), lambda i, f: (i, 0)),
        out_shape=jax.ShapeDtypeStruct(((n_first + n_second) * tm, d), F32),
        scratch_shapes=[pltpu.VMEM((tm, d), BF16)],
        compiler_params=_params("parallel", "arbitrary"),
        name="ffn",
    )(x, *[a for a, _ in second], norm, w_in, w_in, wd)


def _norm_matmul_kernel(x_ref, g_ref, w_ref, o_ref, h_scr):
    @pl.when(pl.program_id(1) == 0)
    def _():
        h_scr[...] = _rms(x_ref[...], g_ref[...]).astype(BF16)

    o_ref[...] = jnp.dot(h_scr[...], w_ref[...], preferred_element_type=F32).astype(o_ref.dtype)


def _norm_matmul(x, g, w, layer, tm, tn, out_dtype, name, row_block0=0, rows=None):
    d = x.shape[1]
    m = x.shape[0] if rows is None else rows
    n = w.shape[2]
    return pl.pallas_call(
        _norm_matmul_kernel,
        grid=(m // tm, n // tn),
        in_specs=[
            pl.BlockSpec((tm, d), lambda i, j: (i + row_block0, 0)),
            pl.BlockSpec((1, d), lambda i, j: (0, 0)),
            pl.BlockSpec((None, d, tn), lambda i, j: (layer, 0, j)),
        ],
        out_specs=pl.BlockSpec((tm, tn), lambda i, j: (i, j)),
        out_shape=jax.ShapeDtypeStruct((m, n), out_dtype),
        scratch_shapes=[pltpu.VMEM((tm, d), BF16)],
        compiler_params=_params("parallel", "arbitrary"),
        name=name,
    )(x, g, w)


def _matmul_norm_res_kernel(y_ref, w_ref, x_ref, g_ref, o_ref):
    d = jnp.dot(y_ref[...], w_ref[...], preferred_element_type=F32)
    o_ref[...] = x_ref[...] + _rms(d, g_ref[...])


def _matmul_norm_res(y, w, layer, x, g, tm, name):
    m, k = y.shape
    n = w.shape[2]
    return pl.pallas_call(
        _matmul_norm_res_kernel,
        grid=(m // tm,),
        in_specs=[
            pl.BlockSpec((tm, k), lambda i: (i, 0)),
            pl.BlockSpec((None, k, n), lambda i: (layer, 0, 0)),
            pl.BlockSpec((tm, n), lambda i: (i, 0)),
            pl.BlockSpec((1, n), lambda i: (0, 0)),
        ],
        out_specs=pl.BlockSpec((tm, n), lambda i: (i, 0)),
        out_shape=jax.ShapeDtypeStruct((m, n), F32),
        compiler_params=_params("parallel"),
        name=name,
    )(y, w, x, g)


def _lru_gates(xc, wg, bg, c):
    g = jnp.dot(xc.astype(BF16), wg, preferred_element_type=F32) + bg
    r = jax.nn.sigmoid(g[:, :LRU_PAIR])
    i = jax.nn.sigmoid(g[:, LRU_PAIR:])
    log_a = c * r
    a = jnp.exp(log_a)
    b = jnp.sqrt(-jnp.tanh(log_a) * (a * a + 1.0)) * (i * xc)
    return a, b


def _layer_norm(x, g, b):
    mu = jnp.mean(x, axis=-1, keepdims=True)
    xc = x - mu
    return (xc * lax.rsqrt(jnp.mean(xc * xc, axis=-1, keepdims=True) + EPS)) * g + b


def _mix_prompt_kernel(x_ref, g_ref, w_ref, cw_ref, cb_ref, wg_ref, bg_ref, lam_ref, ln_ref, sw_ref, sb_ref, ys_ref,
                       y_ref, xlast_ref, hlast_ref, z_a, z_b, xprev_scr, hc_scr, xc_scr,
                       *, nt, n_prompt, tt, dl, ds):
    i = pl.program_id(0)
    t = (i + nt - 1) % nt

    @pl.when(i == 0)
    def _():
        z_b[...] = jnp.zeros_like(z_b)

    @pl.when((t == 0) | (i == 0))
    def _():
        xprev_scr[...] = jnp.zeros_like(xprev_scr)
        hc_scr[...] = jnp.zeros_like(hc_scr)

    def step(z_write, z_read):
        h = _rms(x_ref[...], g_ref[...]).astype(BF16)
        dz = w_ref.shape[1]

        def project(k, of):
            cols = slice(k * dz // of, (k + 1) * dz // of)
            z_write[:, cols] = jnp.dot(h, w_ref[:, cols], preferred_element_type=F32)

        _mixer_prompt_tile(z_read, cw_ref, cb_ref, wg_ref, bg_ref, lam_ref, ln_ref, sw_ref, sb_ref,
                           y_ref, xlast_ref, hlast_ref, xprev_scr, hc_scr, xc_scr, project, tt=tt, dl=dl, ds=ds)

    @pl.when((i <= n_prompt) & (i % 2 == 0))
    def _():
        step(z_a, z_b)

    @pl.when((i <= n_prompt) & (i % 2 == 1))
    def _():
        step(z_b, z_a)

    @pl.when(i > n_prompt)
    def _():
        y_ref[...] = ys_ref[...]


def _mixer_prompt_tile(z_ref, cw_ref, cb_ref, wg_ref, bg_ref, lam_ref, ln_ref, sw_ref, sb_ref,
                       y_ref, xlast_ref, hlast_ref, xprev_scr, hc_scr, xc_scr, project, *, tt, dl, ds):
    ng = tt // SUBLANES
    npairs = dl // LRU_PAIR
    nslices = 2 * npairs
    project(0, nslices)

    x = z_ref[:, 0:dl]
    w = cw_ref[...]
    cb = cb_ref[...]
    acc = cb + w[CONV_WIDTH - 1:CONV_WIDTH, :] * x
    for k in range(1, CONV_WIDTH):
        acc = acc + w[CONV_WIDTH - 1 - k:CONV_WIDTH - k, :] * pltpu.roll(x, k, 0)
    xc_scr[...] = acc
    x_head = x[0:SUBLANES, :]
    ext = jnp.concatenate([xprev_scr[...], x_head], axis=0)
    acc = cb + w[CONV_WIDTH - 1:CONV_WIDTH, :] * x_head
    for k in range(1, CONV_WIDTH):
        acc = acc + w[CONV_WIDTH - 1 - k:CONV_WIDTH - k, :] * pltpu.roll(ext, k, 0)[SUBLANES:, :]
    xc_scr[0:SUBLANES, :] = acc
    x_tail = x[tt - SUBLANES:tt, :]
    xprev_scr[...] = x_tail
    xlast_ref[0] = x_tail

    c_all = -LRU_C * _softplus(-lam_ref[...])
    sidx = lax.broadcasted_iota(jnp.int32, (ng, SUBLANES, LRU_PAIR), 1)
    for p in range(npairs):
        sl = slice(p * LRU_PAIR, (p + 1) * LRU_PAIR)
        project(1 + p, nslices)
        a, b = _lru_gates(xc_scr[:, sl], wg_ref[p], bg_ref[p], c_all[:, sl])
        a = a.reshape(ng, SUBLANES, LRU_PAIR)
        b = b.reshape(ng, SUBLANES, LRU_PAIR)
        shift = 1
        while shift < SUBLANES:
            keep = sidx >= shift
            b = jnp.where(keep, a * pltpu.roll(b, shift, 1) + b, b)
            a = jnp.where(keep, a * pltpu.roll(a, shift, 1), a)
            shift *= 2
        carry = hc_scr[:, sl]
        groups = []
        for gi in range(ng):
            hg = a[gi] * carry + b[gi]
            groups.append(hg)
            carry = hg[SUBLANES - 1:SUBLANES, :]
        hc_scr[:, sl] = carry
        hlast_ref[0, :, sl] = carry
        h_seq = jnp.concatenate(groups, axis=0)
        y_ref[:, sl] = (h_seq * _gelu(z_ref[:, dl + p * LRU_PAIR:dl + (p + 1) * LRU_PAIR])).astype(BF16)

    project(npairs + 1, nslices)
    v = _layer_norm(_gelu(z_ref[:, 2 * dl + ds:2 * dl + 2 * ds]), ln_ref[0:1, :], ln_ref[1:2, :])
    vb = v.astype(BF16)
    project(npairs + 2, nslices)
    u = _gelu(z_ref[:, 2 * dl:2 * dl + ds])
    nc = tt // CHUNK
    hd = ds // SGU_HEADS
    row = lax.broadcasted_iota(jnp.int32, (CHUNK, CHUNK), 0)
    col = lax.broadcasted_iota(jnp.int32, (CHUNK, CHUNK), 1)
    for g in range(SGU_HEADS):
        if g == SGU_HEADS // 2:
            project(npairs + 3, nslices)
        cs = slice(g * hd, (g + 1) * hd)
        wt = jnp.where(row >= col, sw_ref[g], 0.0).astype(BF16)
        vg = jnp.concatenate([vb[c * CHUNK:(c + 1) * CHUNK, cs] for c in range(nc)], axis=1)
        s = jnp.dot(wt, vg, preferred_element_type=F32)
        for c in range(nc):
            rs = slice(c * CHUNK, (c + 1) * CHUNK)
            sc = s[:, c * hd:(c + 1) * hd] + sb_ref[:, cs]
            y_ref[rs, dl + g * hd:dl + (g + 1) * hd] = (u[rs, cs] * sc).astype(BF16)
    assert npairs + 4 == nslices


def _mix_prompt(x, g, w_in, layer, ys, nbatch, seq, cw, cb, wg, bg, lam, ln, sw, sb, tt):
    d = x.shape[1]
    dz = w_in.shape[2]
    dl = cw.shape[1]
    ds = ln.shape[1]
    nt = seq // tt
    n_prompt = nbatch * nt
    n_sample = ys.shape[0] // tt
    full = lambda *shape: pl.BlockSpec(shape, lambda i: (0,) * len(shape))
    batch_of = lambda i: jnp.clip((i - 1) // nt, 0, nbatch - 1)
    return pl.pallas_call(
        functools.partial(_mix_prompt_kernel, nt=nt, n_prompt=n_prompt, tt=tt, dl=dl, ds=ds),
        grid=(n_prompt + 1 + n_sample,),
        in_specs=[
            pl.BlockSpec((tt, d), lambda i: (jnp.minimum(i, n_prompt - 1), 0)),
            full(*g.shape),
            pl.BlockSpec((None, d, dz), lambda i: (layer, 0, 0), pipeline_mode=pl.Buffered(1)),
            full(*cw.shape), full(*cb.shape), full(*wg.shape), full(*bg.shape), full(*lam.shape),
            full(*ln.shape), full(*sw.shape), full(*sb.shape),
            pl.BlockSpec((tt, dl + ds), lambda i: (jnp.clip(i - n_prompt - 1, 0, n_sample - 1), 0)),
        ],
        out_specs=[
            pl.BlockSpec((tt, dl + ds), lambda i: (jnp.maximum(i - 1, 0), 0)),
            pl.BlockSpec((1, SUBLANES, dl), lambda i: (batch_of(i), 0, 0)),
            pl.BlockSpec((1, 1, dl), lambda i: (batch_of(i), 0, 0)),
        ],
        out_shape=[
            jax.ShapeDtypeStruct((nbatch * seq + ys.shape[0], dl + ds), BF16),
            jax.ShapeDtypeStruct((nbatch, SUBLANES, dl), F32),
            jax.ShapeDtypeStruct((nbatch, 1, dl), F32),
        ],
        scratch_shapes=[
            pltpu.VMEM((tt, dz), F32),
            pltpu.VMEM((tt, dz), F32),
            pltpu.VMEM((SUBLANES, dl), F32),
            pltpu.VMEM((1, dl), F32),
            pltpu.VMEM((tt, dl), F32),
        ],
        compiler_params=_params("arbitrary"),
        name="mix_prompt",
    )(x, g, w_in, cw, cb, wg, bg, lam, ln, sw, sb, ys)


def _mixer_sample_kernel(z_ref, cbuf_ref, h0_ref, cw_ref, cb_ref, wg_ref, bg_ref, lam_ref, ln_ref,
                         swx_ref, sbx_ref,
                         y_ref, convnew_ref, hlast_ref, v_ref, xc_scr, *, nb, ts, dl, ds):
    rows = lambda t: slice(t * nb, (t + 1) * nb)
    hist = CONV_WIDTH - 1
    xx = [cbuf_ref[k] for k in range(hist)] + [z_ref[rows(t), 0:dl] for t in range(ts)]
    w = cw_ref[...]
    cb = cb_ref[...]
    for t in range(ts):
        acc = cb + w[0:1, :] * xx[t]
        for k in range(1, CONV_WIDTH):
            acc = acc + w[k:k + 1, :] * xx[t + k]
        xc_scr[rows(t), :] = acc
    for k in range(hist):
        convnew_ref[k] = xx[ts + k]

    c_all = -LRU_C * _softplus(-lam_ref[...])
    for p in range(dl // LRU_PAIR):
        sl = slice(p * LRU_PAIR, (p + 1) * LRU_PAIR)
        a, b = _lru_gates(xc_scr[:, sl], wg_ref[p], bg_ref[p], c_all[:, sl])
        h = h0_ref[:, sl]
        for t in range(ts):
            h = a[rows(t), :] * h + b[rows(t), :]
            gl = z_ref[rows(t), dl + p * LRU_PAIR:dl + (p + 1) * LRU_PAIR]
            y_ref[rows(t), sl] = (h * _gelu(gl)).astype(BF16)
        hlast_ref[:, sl] = h

    vs = []
    for t in range(ts):
        v = _layer_norm(_gelu(z_ref[rows(t), 2 * dl + ds:2 * dl + 2 * ds]), ln_ref[0:1, :], ln_ref[1:2, :])
        v_ref[t] = v
        vs.append(v)
    for t in range(ts):
        s = sbx_ref[t:t + 1, :] + swx_ref[t * ts:t * ts + 1, :] * vs[0]
        for j in range(1, t + 1):
            s = s + swx_ref[t * ts + j:t * ts + j + 1, :] * vs[j]
        u = _gelu(z_ref[rows(t), 2 * dl:2 * dl + ds])
        y_ref[rows(t), dl:dl + ds] = (u * s).astype(BF16)


def _mixer_sample(z, cbuf, h0, cw, cb, wg, bg, lam, ln, swx, sbx, nb, ts):
    dl = cw.shape[1]
    ds = ln.shape[1]
    return pl.pallas_call(
        functools.partial(_mixer_sample_kernel, nb=nb, ts=ts, dl=dl, ds=ds),
        out_shape=[
            jax.ShapeDtypeStruct((ts * nb, dl + ds), BF16),
            jax.ShapeDtypeStruct((CONV_WIDTH - 1, nb, dl), F32),
            jax.ShapeDtypeStruct((nb, dl), F32),
            jax.ShapeDtypeStruct((ts, nb, ds), F32),
        ],
        scratch_shapes=[pltpu.VMEM((ts * nb, dl), F32)],
        compiler_params=pltpu.CompilerParams(vmem_limit_bytes=VMEM_LIMIT_BYTES),
        name="mixer_sample",
    )(z, cbuf, h0, cw, cb, wg, bg, lam, ln, swx, sbx)


def _mem_kv_kernel(x_ref, g_ref, w_ref, kt_ref, vb_ref, ktile_ref, vtile_ref, *, nh):
    mem = x_ref.shape[0]
    d = w_ref.shape[1] // 2
    hd = d // nh
    chunks = hd // LANES
    kv = jnp.dot(_rms(x_ref[...], g_ref[...]).astype(BF16), w_ref[...], preferred_element_type=F32)
    k, v = kv[:, :d], kv[:, d:]
    kt_ref[...] = k.T.astype(BF16)
    vb_ref[...] = v.astype(BF16)
    for tile_ref, val in ((ktile_ref, k), (vtile_ref, v)):
        for h in range(nh):
            for c in range(chunks):
                lanes = slice(h * hd + c * LANES, h * hd + (c + 1) * LANES)
                tile_ref[pl.ds(c * nh + h, mem, stride=nh * chunks), :] = val[:, lanes]


def _mem_kv(memf, g, w_kv, nbatch):
    depth, d, d2 = w_kv.shape
    mem = memf.shape[0] // nbatch
    rows = mem * d // LANES
    per_lb = lambda *blk: pl.BlockSpec((None, None) + blk, lambda l, b: (l, b, 0, 0))
    return pl.pallas_call(
        functools.partial(_mem_kv_kernel, nh=XA_HEADS),
        grid=(depth, nbatch),
        in_specs=[
            pl.BlockSpec((mem, d), lambda l, b: (b, 0)),
            pl.BlockSpec((None, 1, d), lambda l, b: (l, 0, 0)),
            pl.BlockSpec((None, d, d2), lambda l, b: (l, 0, 0)),
        ],
        out_specs=[per_lb(d, mem), per_lb(mem, d), per_lb(rows, LANES), per_lb(rows, LANES)],
        out_shape=[
            jax.ShapeDtypeStruct((depth, nbatch, d, mem), BF16),
            jax.ShapeDtypeStruct((depth, nbatch, mem, d), BF16),
            jax.ShapeDtypeStruct((depth, nbatch, rows, LANES), F32),
            jax.ShapeDtypeStruct((depth, nbatch, rows, LANES), F32),
        ],
        compiler_params=_params("arbitrary", "arbitrary"),
        name="mem_kv",
    )(memf, g, w_kv)


def _tiles_to_heads(t, mem, nh):
    depth, nbatch, rows, _ = t.shape
    chunks = rows // (mem * nh)
    t = t.reshape(depth, nbatch, mem, chunks, nh, LANES)
    return jnp.transpose(t, (0, 1, 2, 4, 3, 5)).reshape(depth, nbatch, mem, nh, chunks * LANES)


def _softmax_rows(s):
    e = jnp.exp(s - jnp.max(s, axis=-1, keepdims=True))
    return e / jnp.sum(e, axis=-1, keepdims=True)


def _attn_prompt_kernel(q_ref, kt_ref, v_ref, os_ref, o_ref, *, n_prompt, hd, scale):
    i = pl.program_id(0)

    @pl.when(i < n_prompt)
    def _():
        for h in range(XA_HEADS):
            hs = slice(h * hd, (h + 1) * hd)
            s = jnp.dot(q_ref[:, hs], kt_ref[0, hs, :], preferred_element_type=F32) * scale
            p = _softmax_rows(s).astype(BF16)
            o_ref[:, hs] = jnp.dot(p, v_ref[0, :, hs], preferred_element_type=F32).astype(BF16)

    @pl.when(i >= n_prompt)
    def _():
        o_ref[...] = os_ref[...]


def _attn_prompt(q, kt, v, layer, o_sample, nbatch, seq, tq):
    d = q.shape[1]
    mem = v.shape[2]
    hd = d // XA_HEADS
    nt = seq // tq
    n_prompt = nbatch * nt
    n_sample = o_sample.shape[0] // tq
    batch_of = lambda i: jnp.minimum(i // nt, nbatch - 1)
    return pl.pallas_call(
        functools.partial(_attn_prompt_kernel, n_prompt=n_prompt, hd=hd, scale=hd ** -0.5),
        grid=(n_prompt + n_sample,),
        in_specs=[
            pl.BlockSpec((tq, d), lambda i: (jnp.minimum(i, n_prompt - 1), 0)),
            pl.BlockSpec((None, 1, d, mem), lambda i: (layer, batch_of(i), 0, 0)),
            pl.BlockSpec((None, 1, mem, d), lambda i: (layer, batch_of(i), 0, 0)),
            pl.BlockSpec((tq, d), lambda i: (jnp.maximum(i - n_prompt, 0), 0)),
        ],
        out_specs=pl.BlockSpec((tq, d), lambda i: (i, 0)),
        out_shape=jax.ShapeDtypeStruct((nbatch * seq + o_sample.shape[0], d), BF16),
        compiler_params=_params("arbitrary"),
        name="attn_prompt",
    )(q, kt, v, o_sample)


def _attn_sample_kernel(qt_ref, k0_ref, k1_ref, v0_ref, v1_ref, o_ref, *, ts, scale, steps_per_group):
    gb, mem = k0_ref.shape[:2]
    nh, hd, ncol = qt_ref.shape[1:]
    sub = pl.program_id(0) % steps_per_group

    @pl.when(sub == 0)
    def _():
        o_ref[...] = jnp.zeros_like(o_ref)

    def head_rows(halves, h):
        flat = [r.reshape(gb, mem * SUBLANES, LANES) for r in halves]
        chunks = hd // LANES
        per_half = chunks // len(halves)
        return jnp.concatenate(
            [jnp.concatenate([flat[c // per_half][j, pl.ds((c % per_half) * nh + h, mem, stride=SUBLANES), :]
                              for c in range(chunks)], axis=1)
             for j in range(gb)], axis=0).astype(BF16)

    col_owner = lax.broadcasted_iota(jnp.int32, (gb, mem, ncol), 2) // ts
    own = col_owner == sub * gb + lax.broadcasted_iota(jnp.int32, (gb, mem, ncol), 0)
    for h in range(nh):
        st = jnp.dot(head_rows((k0_ref, k1_ref), h), qt_ref[0, h], preferred_element_type=F32) * scale
        st = st.reshape(gb, mem, ncol)
        e = jnp.exp(st - jnp.max(st, axis=1, keepdims=True))
        p = e / jnp.sum(e, axis=1, keepdims=True)
        pm = jnp.where(own, p, 0.0).astype(BF16).reshape(gb * mem, ncol)
        o = lax.dot_general(pm, head_rows((v0_ref, v1_ref), h), (((0,), (0,)), ((), ())),
                            preferred_element_type=F32)
        o_ref[0, :, h * hd:(h + 1) * hd] += o


def _attn_sample(qt, k_tiles, v_tiles, layer, ts):
    ngroups, nh, hd, ncol = qt.shape
    _, nb, mem, halves, _, _ = k_tiles.shape
    assert halves == 2 and nh * (hd // LANES) == halves * SUBLANES
    gb = CACHE_STEP_BATCH
    steps_per_group = nb // ngroups // gb

    def half_spec(half):
        return pl.BlockSpec((None, gb, mem, None, SUBLANES, LANES), lambda i: (layer, i, 0, half, 0, 0))

    return pl.pallas_call(
        functools.partial(_attn_sample_kernel, ts=ts, scale=hd ** -0.5, steps_per_group=steps_per_group),
        grid=(nb // gb,),
        in_specs=[
            pl.BlockSpec((1, nh, hd, ncol), lambda i: (i // steps_per_group, 0, 0, 0)),
            half_spec(0), half_spec(1), half_spec(0), half_spec(1),
        ],
        out_specs=pl.BlockSpec((1, ncol, nh * hd), lambda i: (i // steps_per_group, 0, 0)),
        out_shape=jax.ShapeDtypeStruct((ngroups, ncol, nh * hd), F32),
        compiler_params=_params("arbitrary"),
        name="attn_sample",
    )(qt, k_tiles, k_tiles, v_tiles, v_tiles)


def _cache_tiles(cache):
    depth, nb, mem, nh, hd = cache.shape
    c = cache.reshape(depth, nb, mem, nh, hd // LANES, LANES)
    return jnp.transpose(c, (0, 1, 2, 4, 3, 5)).reshape(depth, nb, mem, -1, SUBLANES, LANES)


def _pad_to(a, axis, size):
    pad = [(0, 0)] * a.ndim
    pad[axis] = (0, size - a.shape[axis])
    return jnp.pad(a, pad)


def _gate_weights(w_a, b_a, w_i, b_i):
    heads, hd, _ = w_a.shape
    per = LRU_PAIR // hd
    npair = heads // per

    def blockdiag(w):
        w = w.reshape(npair, per, hd, hd)
        eye = jnp.eye(per, dtype=w.dtype)
        return jnp.einsum('pade,ab->padbe', w, eye).reshape(npair, per * hd, per * hd)

    wg = jnp.concatenate([blockdiag(w_a), blockdiag(w_i)], axis=-1).astype(BF16)
    bg = jnp.concatenate([b_a.reshape(npair, 1, per * hd), b_i.reshape(npair, 1, per * hd)], axis=-1)
    return wg, bg


def kernel(x_prompt, x_sample, mem_prompt, cache_mem_k, cache_mem_v, state_conv, state_lru_h, ffn1_norm, ffn1_w_in, ffn1_w_down, mix_norm, w_in, conv_w, conv_b, lru_w_a, lru_b_a, lru_w_i, lru_b_i, lru_lambda, sgu_ln, sgu_w, sgu_b, w_out, xa_norm, xa_w_q, xa_w_kv, xa_w_o, ffn2_norm, ffn2_w_in, ffn2_w_down):
    nbatch, seq, d = x_prompt.shape
    nb, ts, _ = x_sample.shape
    depth = ffn1_norm.shape[0]
    mem = mem_prompt.shape[1]
    dl = conv_w.shape[-1]
    ds = sgu_ln.shape[-1]
    hd = d // XA_HEADS
    d_ff = ffn1_w_down.shape[1]
    mp, ms = nbatch * seq, nb * ts
    assert seq % CHUNK == 0 and ts < CHUNK and ts >= CONV_WIDTH - 1
    assert nb % SAMPLE_GROUP == 0 and SAMPLE_GROUP * ts <= LANES
    assert dl % LRU_PAIR == 0 and ds % SGU_HEADS == 0

    tm = _pick_tile(math.gcd(mp, ms), (512, 256, 128, 64))
    tt = _pick_tile(math.gcd(seq, ms), (256, 128))
    tq = _pick_tile(math.gcd(seq, ms), (512, 256, 128, 64))
    ngroups = nb // SAMPLE_GROUP

    x_p, x_s = x_prompt.reshape(mp, d), jnp.transpose(x_sample, (1, 0, 2)).reshape(ms, d)

    ffn1_wi, ffn2_wi = ffn1_w_in.astype(BF16), ffn2_w_in.astype(BF16)
    ffn1_wd, ffn2_wd = ffn1_w_down.astype(BF16), ffn2_w_down.astype(BF16)
    w_in_b, w_out_b = w_in.astype(BF16), w_out.astype(BF16)
    w_q_b, w_kv_b, w_o_b = xa_w_q.astype(BF16), xa_w_kv.astype(BF16), xa_w_o.astype(BF16)
    k_tiles, v_tiles = _cache_tiles(cache_mem_k), _cache_tiles(cache_mem_v)

    kt_p, v_p, ktile_p, vtile_p = _mem_kv(mem_prompt.reshape(nbatch * mem, d), xa_norm[:, 2:3], w_kv_b, nbatch)

    outs = {k: [] for k in ("cvp", "hp", "cvs", "hs", "vs")}
    for l in range(depth):
        if l == 0:
            x = _ffn(x_p, ffn1_norm, ffn1_wi, ffn1_wd, l, tm, xs=x_s)
        else:
            x = _ffn(x, ffn1_norm, ffn1_wi, ffn1_wd, l, tm)

        wg, bg = _gate_weights(lru_w_a[l], lru_b_a[l], lru_w_i[l], lru_b_i[l])
        cw, cb, lam = conv_w[l], conv_b[l][None, :], lru_lambda[l][None, :]
        sb_rows = jnp.repeat(sgu_b[l].T, ds // SGU_HEADS, axis=1)
        tri = jnp.tril(jnp.ones((ts, ts), F32))
        swx = jnp.repeat(jnp.transpose(sgu_w[l][:, :ts, :ts] * tri, (1, 2, 0)).reshape(ts * ts, SGU_HEADS),
                         ds // SGU_HEADS, axis=1)
        z_s = _norm_matmul(x, mix_norm[l, 0:1], w_in_b, l, tm, _pick_tile(w_in.shape[2], (2048, 1024)), F32,
                           "mix_in_sample", row_block0=mp // tm, rows=ms)
        y_s, conv_s, h_s, v_s = _mixer_sample(
            z_s, jnp.transpose(state_conv[l], (1, 0, 2)), state_lru_h[l], cw, cb, wg, bg, lam, sgu_ln[l],
            swx, sb_rows[:ts], nb, ts)
        y, xlast, hlast = _mix_prompt(x, mix_norm[l, 0:1], w_in_b, l, y_s, nbatch, seq, cw, cb, wg, bg, lam,
                                      sgu_ln[l], sgu_w[l], sb_rows, tt)
        x = _matmul_norm_res(y, w_out_b, l, x, mix_norm[l, 1:2], tm, "mix_out")
        outs["cvp"].append(xlast[:, SUBLANES - (CONV_WIDTH - 1):, :])
        outs["hp"].append(hlast[:, 0, :])
        outs["cvs"].append(jnp.transpose(conv_s, (1, 0, 2)))
        outs["hs"].append(h_s)
        outs["vs"].append(jnp.transpose(v_s, (1, 0, 2)))

        q = _norm_matmul(x, xa_norm[l, 0:1], w_q_b, l, tm, d, BF16, "xa_q")
        qt = q[mp:].reshape(ts, ngroups, SAMPLE_GROUP, XA_HEADS, hd)
        qt = jnp.transpose(qt, (1, 3, 4, 2, 0)).reshape(ngroups, XA_HEADS, hd, SAMPLE_GROUP * ts)
        o_s = _attn_sample(qt, k_tiles, v_tiles, l, ts).astype(BF16)
        o_s = jnp.transpose(o_s.reshape(ngroups, SAMPLE_GROUP, ts, d), (2, 0, 1, 3)).reshape(ms, d)
        o = _attn_prompt(q, kt_p, v_p, l, o_s, nbatch, seq, tq)
        x = _matmul_norm_res(o, w_o_b, l, x, xa_norm[l, 1:2], tm, "xa_out")

        if l < depth - 1:
            x = _ffn(x, ffn2_norm, ffn2_wi, ffn2_wd, l, tm)

    y_prompt = _ffn(x, ffn2_norm, ffn2_wi, ffn2_wd, depth - 1, tm, rows=mp).reshape(nbatch, seq, d)
    y_sample = _ffn(x, ffn2_norm, ffn2_wi, ffn2_wd, depth - 1, tm, row_block0=mp // tm, rows=ms)
    y_sample = jnp.transpose(y_sample.reshape(ts, nb, d), (1, 0, 2))
    return (y_prompt, y_sample, _tiles_to_heads(ktile_p, mem, XA_HEADS), _tiles_to_heads(vtile_p, mem, XA_HEADS),
            jnp.stack(outs["cvp"]),
            jnp.stack(outs["hp"]), jnp.stack(outs["cvs"]), jnp.stack(outs["hs"]), jnp.stack(outs["vs"]))
```

```python
import functools
import math

import jax
import jax.numpy as jnp
from jax import lax
from jax.experimental import pallas as pl
from jax.experimental.pallas import tpu as pltpu

F32 = jnp.float32
BF16 = jnp.bfloat16

EPS = 1e-6
HALF = 0.5
LRU_C = 8.0
CONV_WIDTH = 4
CHUNK = 128
LRU_HEADS = 8
SGU_HEADS = 8
XA_HEADS = 4

VMEM_LIMIT_BYTES = 56 * 1024 * 1024
SUBLANES = 8
LANES = 128
BF16_ROWS = 16
FF_TILE = 1024
SAMPLE_GROUP = 8
CACHE_STEP_BATCH = 4
LRU_PAIR = 2 * LANES


def _params(*sem):
    return pltpu.CompilerParams(dimension_semantics=sem, vmem_limit_bytes=VMEM_LIMIT_BYTES)


def _pick_tile(n, prefs):
    for p in prefs:
        if n % p == 0:
            return p
    raise ValueError(f"no tile in {prefs} divides {n}")


def _rms(x, g):
    return (x * lax.rsqrt(jnp.mean(x * x, axis=-1, keepdims=True) + EPS)) * g


def _gelu(x):
    c = math.sqrt(2.0 / math.pi)
    return 0.5 * x * (1.0 + jnp.tanh(c * (x + 0.044715 * (x * x * x))))


def _softplus(x):
    return jnp.maximum(x, 0.0) + jnp.log1p(jnp.exp(-jnp.abs(x)))


def _ffn_kernel(x_ref, *refs, nf, last_cols, n_first, two_sources, ncast):
    i, f = pl.program_id(0), pl.program_id(1)
    refs = list(refs)
    xs_ref = refs.pop(0) if two_sources else None
    g_ref, wg_ref, wu_ref, wd_ref = refs[:4]
    cast_in, o_ref, cast_out, h_scr = refs[4:4 + ncast], refs[4 + ncast], refs[5 + ncast:5 + 2 * ncast], refs[-1]
    if two_sources:
        read_x = lambda: jnp.where(i < n_first, x_ref[...], xs_ref[...])
    else:
        read_x = lambda: x_ref[...]

    @pl.when(f == 0)
    def _():
        h_scr[...] = _rms(read_x(), g_ref[0:1, :]).astype(BF16)
        o_ref[...] = jnp.zeros_like(o_ref)

    def accumulate(cols):
        h = h_scr[...]
        gate = jnp.dot(h, wg_ref[:, :cols], preferred_element_type=F32)
        up = jnp.dot(h, wu_ref[0, :, FF_TILE - cols:], preferred_element_type=F32)
        act = ((gate * jax.nn.sigmoid(gate)) * up).astype(BF16)
        o_ref[...] += jnp.dot(act, wd_ref[:cols, :], preferred_element_type=F32)
        for src, dst in zip(cast_in, cast_out):
            dst[...] = src[...].astype(BF16)

    if last_cols == FF_TILE:
        accumulate(FF_TILE)
    else:
        @pl.when(f < nf - 1)
        def _():
            accumulate(FF_TILE)

        @pl.when(f == nf - 1)
        def _():
            accumulate(last_cols)

    @pl.when(f == nf - 1)
    def _():
        o_ref[...] = read_x() + HALF * _rms(o_ref[...], g_ref[1:2, :])


def _cast_blocking(shape, nsteps):
    r, c = shape
    ncb = 2 if c > 4096 and (c // 2) % LANES == 0 else 1
    br = -(-(-(-r * ncb // nsteps)) // BF16_ROWS) * BF16_ROWS
    assert -(-r // br) * ncb <= nsteps
    return br, c // ncb, ncb


def _ffn(x, norm, w_in, wd, layer, tm, xs=None, row_block0=0, rows=None, cast=()):
    d = x.shape[1]
    n_first = (x.shape[0] if rows is None else rows) // tm
    n_second = 0 if xs is None else xs.shape[0] // tm
    second = [] if xs is None else [(xs, pl.BlockSpec((tm, d), lambda i, f: (jnp.maximum(i - n_first, 0), 0)))]
    ff = wd.shape[1]
    assert ff % LANES == 0 and FF_TILE % LANES == 0 and ff >= FF_TILE
    nf = pl.cdiv(ff, FF_TILE)
    up_start = lambda f: jnp.minimum(ff // LANES + f * (FF_TILE // LANES), (2 * ff - FF_TILE) // LANES) * LANES
    nsteps = (n_first + n_second) * nf
    cast_in_specs, cast_out_specs = [], []
    for a, src_layer in cast:
        br, bc, ncb = _cast_blocking(a.shape[1:], nsteps)
        last = -(-a.shape[1] // br) * ncb - 1
        blk = lambda i, f, ncb=ncb, last=last: (jnp.minimum(i * nf + f, last) // ncb,
                                                jnp.minimum(i * nf + f, last) % ncb)
        cast_in_specs.append(pl.BlockSpec((None, br, bc), lambda i, f, blk=blk, sl=src_layer: (sl,) + blk(i, f)))
        cast_out_specs.append(pl.BlockSpec((br, bc), blk))
    res = pl.pallas_call(
        functools.partial(_ffn_kernel, nf=nf, last_cols=ff - (nf - 1) * FF_TILE, n_first=n_first,
                          two_sources=xs is not None, ncast=len(cast)),
        grid=(n_first + n_second, nf),
        in_specs=[
            pl.BlockSpec((tm, d), lambda i, f: (jnp.minimum(i, n_first - 1) + row_block0, 0)),
            *[spec for _, spec in second],
            pl.BlockSpec((None, 2, d), lambda i, f: (layer, 0, 0)),
            pl.BlockSpec((None, d, FF_TILE), lambda i, f: (layer, 0, f)),
            pl.BlockSpec((pl.Element(1), pl.Element(d), pl.Element(FF_TILE)), lambda i, f: (layer, 0, up_start(f))),
            pl.BlockSpec((None, FF_TILE, d), lambda i, f: (layer, f, 0)),
            *cast_in_specs,
        ],
        out_specs=[pl.BlockSpec((tm, d), lambda i, f: (i, 0)), *cast_out_specs],
        out_shape=[jax.ShapeDtypeStruct(((n_first + n_second) * tm, d), F32),
                   *[jax.ShapeDtypeStruct(a.shape[1:], BF16) for a, _ in cast]],
        scratch_shapes=[pltpu.VMEM((tm, d), BF16)],
        compiler_params=_params("arbitrary", "arbitrary"),
        name="ffn",
    )(x, *[a for a, _ in second], norm, w_in, w_in, wd, *[a for a, _ in cast])
    return res if cast else res[0]


def _norm_matmul_kernel(x_ref, g_ref, w_ref, o_ref, h_scr):
    @pl.when(pl.program_id(1) == 0)
    def _():
        h_scr[...] = _rms(x_ref[...], g_ref[...]).astype(BF16)

    o_ref[...] = jnp.dot(h_scr[...], w_ref[...], preferred_element_type=F32).astype(o_ref.dtype)


def _norm_matmul(x, g, w, layer, tm, tn, out_dtype, name, row_block0=0, rows=None):
    d = x.shape[1]
    m = x.shape[0] if rows is None else rows
    n = w.shape[2]
    return pl.pallas_call(
        _norm_matmul_kernel,
        grid=(m // tm, n // tn),
        in_specs=[
            pl.BlockSpec((tm, d), lambda i, j: (i + row_block0, 0)),
            pl.BlockSpec((1, d), lambda i, j: (0, 0)),
            pl.BlockSpec((None, d, tn), lambda i, j: (layer, 0, j)),
        ],
        out_specs=pl.BlockSpec((tm, tn), lambda i, j: (i, j)),
        out_shape=jax.ShapeDtypeStruct((m, n), out_dtype),
        scratch_shapes=[pltpu.VMEM((tm, d), BF16)],
        compiler_params=_params("parallel", "arbitrary"),
        name=name,
    )(x, g, w)


def _matmul_norm_res_kernel(y_ref, w_ref, x_ref, g_ref, o_ref):
    d = jnp.dot(y_ref[...], w_ref[...], preferred_element_type=F32)
    o_ref[...] = x_ref[...] + _rms(d, g_ref[...])


def _matmul_norm_res(y, w, layer, x, g, tm, name):
    m, k = y.shape
    n = w.shape[2]
    return pl.pallas_call(
        _matmul_norm_res_kernel,
        grid=(m // tm,),
        in_specs=[
            pl.BlockSpec((tm, k), lambda i: (i, 0)),
            pl.BlockSpec((None, k, n), lambda i: (layer, 0, 0)),
            pl.BlockSpec((tm, n), lambda i: (i, 0)),
            pl.BlockSpec((1, n), lambda i: (0, 0)),
        ],
        out_specs=pl.BlockSpec((tm, n), lambda i: (i, 0)),
        out_shape=jax.ShapeDtypeStruct((m, n), F32),
        compiler_params=_params("parallel"),
        name=name,
    )(y, w, x, g)


def _lru_gates(xc, wg, bg, c):
    g = jnp.dot(xc.astype(BF16), wg, preferred_element_type=F32) + bg
    r = jax.nn.sigmoid(g[:, :LRU_PAIR])
    i = jax.nn.sigmoid(g[:, LRU_PAIR:])
    log_a = c * r
    a = jnp.exp(log_a)
    b = jnp.sqrt(-jnp.tanh(log_a) * (a * a + 1.0)) * (i * xc)
    return a, b


def _layer_norm(x, g, b):
    mu = jnp.mean(x, axis=-1, keepdims=True)
    xc = x - mu
    return (xc * lax.rsqrt(jnp.mean(xc * xc, axis=-1, keepdims=True) + EPS)) * g + b


def _mix_prompt_kernel(x_ref, g_ref, w_ref, cw_ref, cb_ref, wg_ref, bg_ref, lam_ref, ln_ref, sw_ref, sb_ref, ys_ref,
                       y_ref, xlast_ref, hlast_ref, z_a, z_b, xprev_scr, hc_scr, xc_scr,
                       *, nt, n_prompt, tt, dl, ds):
    i = pl.program_id(0)
    t = (i + nt - 1) % nt

    @pl.when(i == 0)
    def _():
        z_b[...] = jnp.zeros_like(z_b)

    @pl.when((t == 0) | (i == 0))
    def _():
        xprev_scr[...] = jnp.zeros_like(xprev_scr)
        hc_scr[...] = jnp.zeros_like(hc_scr)

    def step(z_write, z_read):
        h = _rms(x_ref[...], g_ref[...]).astype(BF16)
        dz = w_ref.shape[1]

        def project(k, of):
            cols = slice(k * dz // of, (k + 1) * dz // of)
            z_write[:, cols] = jnp.dot(h, w_ref[:, cols], preferred_element_type=F32)

        _mixer_prompt_tile(z_read, cw_ref, cb_ref, wg_ref, bg_ref, lam_ref, ln_ref, sw_ref, sb_ref,
                           y_ref, xlast_ref, hlast_ref, xprev_scr, hc_scr, xc_scr, project, tt=tt, dl=dl, ds=ds)

    @pl.when((i <= n_prompt) & (i % 2 == 0))
    def _():
        step(z_a, z_b)

    @pl.when((i <= n_prompt) & (i % 2 == 1))
    def _():
        step(z_b, z_a)

    @pl.when(i > n_prompt)
    def _():
        y_ref[...] = ys_ref[...]


def _mixer_prompt_tile(z_ref, cw_ref, cb_ref, wg_ref, bg_ref, lam_ref, ln_ref, sw_ref, sb_ref,
                       y_ref, xlast_ref, hlast_ref, xprev_scr, hc_scr, xc_scr, project, *, tt, dl, ds):
    ng = tt // SUBLANES
    npairs = dl // LRU_PAIR
    nslices = 4 * npairs
    project(0, nslices)
    project(1, nslices)

    x = z_ref[:, 0:dl]
    w = cw_ref[...]
    cb = cb_ref[...]
    acc = cb + w[CONV_WIDTH - 1:CONV_WIDTH, :] * x
    for k in range(1, CONV_WIDTH):
        acc = acc + w[CONV_WIDTH - 1 - k:CONV_WIDTH - k, :] * pltpu.roll(x, k, 0)
    xc_scr[...] = acc
    x_head = x[0:SUBLANES, :]
    ext = jnp.concatenate([xprev_scr[...], x_head], axis=0)
    acc = cb + w[CONV_WIDTH - 1:CONV_WIDTH, :] * x_head
    for k in range(1, CONV_WIDTH):
        acc = acc + w[CONV_WIDTH - 1 - k:CONV_WIDTH - k, :] * pltpu.roll(ext, k, 0)[SUBLANES:, :]
    xc_scr[0:SUBLANES, :] = acc
    x_tail = x[tt - SUBLANES:tt, :]
    xprev_scr[...] = x_tail
    xlast_ref[0] = x_tail

    c_all = -LRU_C * _softplus(-lam_ref[...])
    sidx = lax.broadcasted_iota(jnp.int32, (ng, SUBLANES, LRU_PAIR), 1)
    for p in range(npairs):
        sl = slice(p * LRU_PAIR, (p + 1) * LRU_PAIR)
        project(2 + 2 * p, nslices)
        a, b = _lru_gates(xc_scr[:, sl], wg_ref[p], bg_ref[p], c_all[:, sl])
        a = a.reshape(ng, SUBLANES, LRU_PAIR)
        b = b.reshape(ng, SUBLANES, LRU_PAIR)
        shift = 1
        while shift < SUBLANES:
            keep = sidx >= shift
            b = jnp.where(keep, a * pltpu.roll(b, shift, 1) + b, b)
            a = jnp.where(keep, a * pltpu.roll(a, shift, 1), a)
            shift *= 2
        project(3 + 2 * p, nslices)
        carry = hc_scr[:, sl]
        groups = []
        for gi in range(ng):
            hg = a[gi] * carry + b[gi]
            groups.append(hg)
            carry = hg[SUBLANES - 1:SUBLANES, :]
        hc_scr[:, sl] = carry
        hlast_ref[0, :, sl] = carry
        h_seq = jnp.concatenate(groups, axis=0)
        y_ref[:, sl] = (h_seq * _gelu(z_ref[:, dl + p * LRU_PAIR:dl + (p + 1) * LRU_PAIR])).astype(BF16)

    project(2 * npairs + 2, nslices)
    project(2 * npairs + 3, nslices)
    v = _layer_norm(_gelu(z_ref[:, 2 * dl + ds:2 * dl + 2 * ds]), ln_ref[0:1, :], ln_ref[1:2, :])
    vb = v.astype(BF16)
    project(2 * npairs + 4, nslices)
    project(2 * npairs + 5, nslices)
    u = _gelu(z_ref[:, 2 * dl:2 * dl + ds])
    nc = tt // CHUNK
    hd = ds // SGU_HEADS
    row = lax.broadcasted_iota(jnp.int32, (CHUNK, CHUNK), 0)
    col = lax.broadcasted_iota(jnp.int32, (CHUNK, CHUNK), 1)
    for g in range(SGU_HEADS):
        if g == SGU_HEADS // 2:
            project(2 * npairs + 6, nslices)
            project(2 * npairs + 7, nslices)
        cs = slice(g * hd, (g + 1) * hd)
        wt = jnp.where(row >= col, sw_ref[g], 0.0).astype(BF16)
        vg = jnp.concatenate([vb[c * CHUNK:(c + 1) * CHUNK, cs] for c in range(nc)], axis=1)
        s = jnp.dot(wt, vg, preferred_element_type=F32)
        for c in range(nc):
            rs = slice(c * CHUNK, (c + 1) * CHUNK)
            sc = s[:, c * hd:(c + 1) * hd] + sb_ref[:, cs]
            y_ref[rs, dl + g * hd:dl + (g + 1) * hd] = (u[rs, cs] * sc).astype(BF16)
    assert 2 * npairs + 8 == nslices


def _mix_prompt(x, g, w_in, layer, ys, nbatch, seq, cw, cb, wg, bg, lam, ln, sw, sb, tt):
    d = x.shape[1]
    dz = w_in.shape[2]
    dl = cw.shape[1]
    ds = ln.shape[1]
    nt = seq // tt
    n_prompt = nbatch * nt
    n_sample = ys.shape[0] // tt
    full = lambda *shape: pl.BlockSpec(shape, lambda i: (0,) * len(shape))
    batch_of = lambda i: jnp.clip((i - 1) // nt, 0, nbatch - 1)
    return pl.pallas_call(
        functools.partial(_mix_prompt_kernel, nt=nt, n_prompt=n_prompt, tt=tt, dl=dl, ds=ds),
        grid=(n_prompt + 1 + n_sample,),
        in_specs=[
            pl.BlockSpec((tt, d), lambda i: (jnp.minimum(i, n_prompt - 1), 0)),
            full(*g.shape),
            pl.BlockSpec((None, d, dz), lambda i: (layer, 0, 0), pipeline_mode=pl.Buffered(1)),
            full(*cw.shape), full(*cb.shape), full(*wg.shape), full(*bg.shape), full(*lam.shape),
            full(*ln.shape), full(*sw.shape), full(*sb.shape),
            pl.BlockSpec((tt, dl + ds), lambda i: (jnp.clip(i - n_prompt - 1, 0, n_sample - 1), 0)),
        ],
        out_specs=[
            pl.BlockSpec((tt, dl + ds), lambda i: (jnp.maximum(i - 1, 0), 0)),
            pl.BlockSpec((1, SUBLANES, dl), lambda i: (batch_of(i), 0, 0)),
            pl.BlockSpec((1, 1, dl), lambda i: (batch_of(i), 0, 0)),
        ],
        out_shape=[
            jax.ShapeDtypeStruct((nbatch * seq + ys.shape[0], dl + ds), BF16),
            jax.ShapeDtypeStruct((nbatch, SUBLANES, dl), F32),
            jax.ShapeDtypeStruct((nbatch, 1, dl), F32),
        ],
        scratch_shapes=[
            pltpu.VMEM((tt, dz), F32),
            pltpu.VMEM((tt, dz), F32),
            pltpu.VMEM((SUBLANES, dl), F32),
            pltpu.VMEM((1, dl), F32),
            pltpu.VMEM((tt, dl), F32),
        ],
        compiler_params=_params("arbitrary"),
        name="mix_prompt",
    )(x, g, w_in, cw, cb, wg, bg, lam, ln, sw, sb, ys)


def _mixer_sample_kernel(z_ref, cbuf_ref, h0_ref, cw_ref, cb_ref, wg_ref, bg_ref, lam_ref, ln_ref,
                         swx_ref, sbx_ref,
                         y_ref, convnew_ref, hlast_ref, v_ref, xc_scr, *, nb, ts, dl, ds):
    rows = lambda t: slice(t * nb, (t + 1) * nb)
    hist = CONV_WIDTH - 1
    xx = [cbuf_ref[k] for k in range(hist)] + [z_ref[rows(t), 0:dl] for t in range(ts)]
    w = cw_ref[...]
    cb = cb_ref[...]
    for t in range(ts):
        acc = cb + w[0:1, :] * xx[t]
        for k in range(1, CONV_WIDTH):
            acc = acc + w[k:k + 1, :] * xx[t + k]
        xc_scr[rows(t), :] = acc
    for k in range(hist):
        convnew_ref[k] = xx[ts + k]

    c_all = -LRU_C * _softplus(-lam_ref[...])
    for p in range(dl // LRU_PAIR):
        sl = slice(p * LRU_PAIR, (p + 1) * LRU_PAIR)
        a, b = _lru_gates(xc_scr[:, sl], wg_ref[p], bg_ref[p], c_all[:, sl])
        h = h0_ref[:, sl]
        for t in range(ts):
            h = a[rows(t), :] * h + b[rows(t), :]
            gl = z_ref[rows(t), dl + p * LRU_PAIR:dl + (p + 1) * LRU_PAIR]
            y_ref[rows(t), sl] = (h * _gelu(gl)).astype(BF16)
        hlast_ref[:, sl] = h

    vs = []
    for t in range(ts):
        v = _layer_norm(_gelu(z_ref[rows(t), 2 * dl + ds:2 * dl + 2 * ds]), ln_ref[0:1, :], ln_ref[1:2, :])
        v_ref[t] = v
        vs.append(v)
    for t in range(ts):
        s = sbx_ref[t:t + 1, :] + swx_ref[t * ts:t * ts + 1, :] * vs[0]
        for j in range(1, t + 1):
            s = s + swx_ref[t * ts + j:t * ts + j + 1, :] * vs[j]
        u = _gelu(z_ref[rows(t), 2 * dl:2 * dl + ds])
        y_ref[rows(t), dl:dl + ds] = (u * s).astype(BF16)


def _mixer_sample(z, cbuf, h0, cw, cb, wg, bg, lam, ln, swx, sbx, nb, ts):
    dl = cw.shape[1]
    ds = ln.shape[1]
    return pl.pallas_call(
        functools.partial(_mixer_sample_kernel, nb=nb, ts=ts, dl=dl, ds=ds),
        out_shape=[
            jax.ShapeDtypeStruct((ts * nb, dl + ds), BF16),
            jax.ShapeDtypeStruct((CONV_WIDTH - 1, nb, dl), F32),
            jax.ShapeDtypeStruct((nb, dl), F32),
            jax.ShapeDtypeStruct((ts, nb, ds), F32),
        ],
        scratch_shapes=[pltpu.VMEM((ts * nb, dl), F32)],
        compiler_params=pltpu.CompilerParams(vmem_limit_bytes=VMEM_LIMIT_BYTES),
        name="mixer_sample",
    )(z, cbuf, h0, cw, cb, wg, bg, lam, ln, swx, sbx)


def _mem_kv_kernel(x_ref, g_ref, w_ref, kt_ref, vb_ref, ktile_ref, vtile_ref, *, nh):
    mem = x_ref.shape[0]
    d = w_ref.shape[1] // 2
    hd = d // nh
    chunks = hd // LANES
    kv = jnp.dot(_rms(x_ref[...], g_ref[...]).astype(BF16), w_ref[...], preferred_element_type=F32)
    k, v = kv[:, :d], kv[:, d:]
    kt_ref[...] = k.T.astype(BF16)
    vb_ref[...] = v.astype(BF16)
    for tile_ref, val in ((ktile_ref, k), (vtile_ref, v)):
        for h in range(nh):
            for c in range(chunks):
                lanes = slice(h * hd + c * LANES, h * hd + (c + 1) * LANES)
                tile_ref[pl.ds(c * nh + h, mem, stride=nh * chunks), :] = val[:, lanes]


def _mem_kv(memf, g, w_kv, nbatch):
    depth, d, d2 = w_kv.shape
    mem = memf.shape[0] // nbatch
    rows = mem * d // LANES
    per_lb = lambda *blk: pl.BlockSpec((None, None) + blk, lambda l, b: (l, b, 0, 0))
    return pl.pallas_call(
        functools.partial(_mem_kv_kernel, nh=XA_HEADS),
        grid=(depth, nbatch),
        in_specs=[
            pl.BlockSpec((mem, d), lambda l, b: (b, 0)),
            pl.BlockSpec((None, 1, d), lambda l, b: (l, 0, 0)),
            pl.BlockSpec((None, d, d2), lambda l, b: (l, 0, 0)),
        ],
        out_specs=[per_lb(d, mem), per_lb(mem, d), per_lb(rows, LANES), per_lb(rows, LANES)],
        out_shape=[
            jax.ShapeDtypeStruct((depth, nbatch, d, mem), BF16),
            jax.ShapeDtypeStruct((depth, nbatch, mem, d), BF16),
            jax.ShapeDtypeStruct((depth, nbatch, rows, LANES), F32),
            jax.ShapeDtypeStruct((depth, nbatch, rows, LANES), F32),
        ],
        compiler_params=_params("arbitrary", "arbitrary"),
        name="mem_kv",
    )(memf, g, w_kv)


def _tiles_to_heads(t, mem, nh):
    depth, nbatch, rows, _ = t.shape
    chunks = rows // (mem * nh)
    t = t.reshape(depth, nbatch, mem, chunks, nh, LANES)
    return jnp.transpose(t, (0, 1, 2, 4, 3, 5)).reshape(depth, nbatch, mem, nh, chunks * LANES)


def _softmax_rows(s):
    e = jnp.exp(s - jnp.max(s, axis=-1, keepdims=True))
    return e / jnp.sum(e, axis=-1, keepdims=True)


def _attn_prompt_kernel(q_ref, kt_ref, v_ref, os_ref, o_ref, *, n_prompt, hd, scale):
    i = pl.program_id(0)

    @pl.when(i < n_prompt)
    def _():
        for h in range(XA_HEADS):
            hs = slice(h * hd, (h + 1) * hd)
            s = jnp.dot(q_ref[:, hs], kt_ref[0, hs, :], preferred_element_type=F32) * scale
            p = _softmax_rows(s).astype(BF16)
            o_ref[:, hs] = jnp.dot(p, v_ref[0, :, hs], preferred_element_type=F32).astype(BF16)

    @pl.when(i >= n_prompt)
    def _():
        o_ref[...] = os_ref[...]


def _attn_prompt(q, kt, v, layer, o_sample, nbatch, seq, tq):
    d = q.shape[1]
    mem = v.shape[2]
    hd = d // XA_HEADS
    nt = seq // tq
    n_prompt = nbatch * nt
    n_sample = o_sample.shape[0] // tq
    batch_of = lambda i: jnp.minimum(i // nt, nbatch - 1)
    return pl.pallas_call(
        functools.partial(_attn_prompt_kernel, n_prompt=n_prompt, hd=hd, scale=hd ** -0.5),
        grid=(n_prompt + n_sample,),
        in_specs=[
            pl.BlockSpec((tq, d), lambda i: (jnp.minimum(i, n_prompt - 1), 0)),
            pl.BlockSpec((None, 1, d, mem), lambda i: (layer, batch_of(i), 0, 0)),
            pl.BlockSpec((None, 1, mem, d), lambda i: (layer, batch_of(i), 0, 0)),
            pl.BlockSpec((tq, d), lambda i: (jnp.maximum(i - n_prompt, 0), 0)),
        ],
        out_specs=pl.BlockSpec((tq, d), lambda i: (i, 0)),
        out_shape=jax.ShapeDtypeStruct((nbatch * seq + o_sample.shape[0], d), BF16),
        compiler_params=_params("arbitrary"),
        name="attn_prompt",
    )(q, kt, v, o_sample)


def _attn_sample_kernel(qt_ref, k0_ref, k1_ref, v0_ref, v1_ref, o_ref, *, ts, scale, steps_per_group):
    gb, mem = k0_ref.shape[:2]
    nh, hd, ncol = qt_ref.shape[1:]
    sub = pl.program_id(0) % steps_per_group

    @pl.when(sub == 0)
    def _():
        o_ref[...] = jnp.zeros_like(o_ref)

    def head_rows(halves, h):
        flat = [r.reshape(gb, mem * SUBLANES, LANES) for r in halves]
        chunks = hd // LANES
        per_half = chunks // len(halves)
        return jnp.concatenate(
            [jnp.concatenate([flat[c // per_half][j, pl.ds((c % per_half) * nh + h, mem, stride=SUBLANES), :]
                              for c in range(chunks)], axis=1)
             for j in range(gb)], axis=0).astype(BF16)

    col_owner = lax.broadcasted_iota(jnp.int32, (gb, mem, ncol), 2) // ts
    own = col_owner == sub * gb + lax.broadcasted_iota(jnp.int32, (gb, mem, ncol), 0)
    for h in range(nh):
        st = jnp.dot(head_rows((k0_ref, k1_ref), h), qt_ref[0, h], preferred_element_type=F32) * scale
        st = st.reshape(gb, mem, ncol)
        e = jnp.exp(st - jnp.max(st, axis=1, keepdims=True))
        p = e / jnp.sum(e, axis=1, keepdims=True)
        pm = jnp.where(own, p, 0.0).astype(BF16).reshape(gb * mem, ncol)
        o = lax.dot_general(pm, head_rows((v0_ref, v1_ref), h), (((0,), (0,)), ((), ())),
                            preferred_element_type=F32)
        o_ref[0, :, h * hd:(h + 1) * hd] += o


def _attn_sample(qt, k_tiles, v_tiles, layer, ts):
    ngroups, nh, hd, ncol = qt.shape
    _, nb, mem, halves, _, _ = k_tiles.shape
    assert halves == 2 and nh * (hd // LANES) == halves * SUBLANES
    gb = CACHE_STEP_BATCH
    steps_per_group = nb // ngroups // gb

    def half_spec(half):
        return pl.BlockSpec((None, gb, mem, None, SUBLANES, LANES), lambda i: (layer, i, 0, half, 0, 0))

    return pl.pallas_call(
        functools.partial(_attn_sample_kernel, ts=ts, scale=hd ** -0.5, steps_per_group=steps_per_group),
        grid=(nb // gb,),
        in_specs=[
            pl.BlockSpec((1, nh, hd, ncol), lambda i: (i // steps_per_group, 0, 0, 0)),
            half_spec(0), half_spec(1), half_spec(0), half_spec(1),
        ],
        out_specs=pl.BlockSpec((1, ncol, nh * hd), lambda i: (i // steps_per_group, 0, 0)),
        out_shape=jax.ShapeDtypeStruct((ngroups, ncol, nh * hd), F32),
        compiler_params=_params("arbitrary"),
        name="attn_sample",
    )(qt, k_tiles, k_tiles, v_tiles, v_tiles)


def _cache_tiles(cache):
    depth, nb, mem, nh, hd = cache.shape
    c = cache.reshape(depth, nb, mem, nh, hd // LANES, LANES)
    return jnp.transpose(c, (0, 1, 2, 4, 3, 5)).reshape(depth, nb, mem, -1, SUBLANES, LANES)


def _pad_to(a, axis, size):
    pad = [(0, 0)] * a.ndim
    pad[axis] = (0, size - a.shape[axis])
    return jnp.pad(a, pad)


def _gate_weights(w_a, b_a, w_i, b_i):
    heads, hd, _ = w_a.shape
    per = LRU_PAIR // hd
    npair = heads // per

    def blockdiag(w):
        w = w.reshape(npair, per, hd, hd)
        eye = jnp.eye(per, dtype=w.dtype)
        return jnp.einsum('pade,ab->padbe', w, eye).reshape(npair, per * hd, per * hd)

    wg = jnp.concatenate([blockdiag(w_a), blockdiag(w_i)], axis=-1).astype(BF16)
    bg = jnp.concatenate([b_a.reshape(npair, 1, per * hd), b_i.reshape(npair, 1, per * hd)], axis=-1)
    return wg, bg


def kernel(x_prompt, x_sample, mem_prompt, cache_mem_k, cache_mem_v, state_conv, state_lru_h, ffn1_norm, ffn1_w_in, ffn1_w_down, mix_norm, w_in, conv_w, conv_b, lru_w_a, lru_b_a, lru_w_i, lru_b_i, lru_lambda, sgu_ln, sgu_w, sgu_b, w_out, xa_norm, xa_w_q, xa_w_kv, xa_w_o, ffn2_norm, ffn2_w_in, ffn2_w_down):
    nbatch, seq, d = x_prompt.shape
    nb, ts, _ = x_sample.shape
    depth = ffn1_norm.shape[0]
    mem = mem_prompt.shape[1]
    dl = conv_w.shape[-1]
    ds = sgu_ln.shape[-1]
    hd = d // XA_HEADS
    d_ff = ffn1_w_down.shape[1]
    mp, ms = nbatch * seq, nb * ts
    assert seq % CHUNK == 0 and ts < CHUNK and ts >= CONV_WIDTH - 1
    assert nb % SAMPLE_GROUP == 0 and SAMPLE_GROUP * ts <= LANES
    assert dl % LRU_PAIR == 0 and ds % SGU_HEADS == 0

    tm = _pick_tile(math.gcd(mp, ms), (512, 256, 128, 64))
    tt = _pick_tile(math.gcd(seq, ms), (256, 128))
    tq = _pick_tile(math.gcd(seq, ms), (512, 256, 128, 64))
    ngroups = nb // SAMPLE_GROUP

    x_p, x_s = x_prompt.reshape(mp, d), jnp.transpose(x_sample, (1, 0, 2)).reshape(ms, d)

    ffn_w = [a[0:1].astype(BF16) for a in (ffn1_w_in, ffn1_w_down, ffn2_w_in, ffn2_w_down)]
    w_in_b, w_out_b = w_in.astype(BF16), w_out.astype(BF16)
    w_q_b, w_kv_b, w_o_b = xa_w_q.astype(BF16), xa_w_kv.astype(BF16), xa_w_o.astype(BF16)
    k_tiles, v_tiles = _cache_tiles(cache_mem_k), _cache_tiles(cache_mem_v)

    kt_p, v_p, ktile_p, vtile_p = _mem_kv(mem_prompt.reshape(nbatch * mem, d), xa_norm[:, 2:3], w_kv_b, nbatch)

    outs = {k: [] for k in ("cvp", "hp", "cvs", "hs", "vs")}
    for l in range(depth):
        ffn1_wi, ffn1_wd, ffn2_wi, ffn2_wd = ffn_w
        if l == 0:
            x = _ffn(x_p, ffn1_norm[l:l + 1], ffn1_wi, ffn1_wd, 0, tm, xs=x_s)
        else:
            x = _ffn(x, ffn1_norm[l:l + 1], ffn1_wi, ffn1_wd, 0, tm)

        wg, bg = _gate_weights(lru_w_a[l], lru_b_a[l], lru_w_i[l], lru_b_i[l])
        cw, cb, lam = conv_w[l], conv_b[l][None, :], lru_lambda[l][None, :]
        sb_rows = jnp.repeat(sgu_b[l].T, ds // SGU_HEADS, axis=1)
        tri = jnp.tril(jnp.ones((ts, ts), F32))
        swx = jnp.repeat(jnp.transpose(sgu_w[l][:, :ts, :ts] * tri, (1, 2, 0)).reshape(ts * ts, SGU_HEADS),
                         ds // SGU_HEADS, axis=1)
        z_s = _norm_matmul(x, mix_norm[l, 0:1], w_in_b, l, tm, _pick_tile(w_in.shape[2], (2048, 1024)), F32,
                           "mix_in_sample", row_block0=mp // tm, rows=ms)
        y_s, conv_s, h_s, v_s = _mixer_sample(
            z_s, jnp.transpose(state_conv[l], (1, 0, 2)), state_lru_h[l], cw, cb, wg, bg, lam, sgu_ln[l],
            swx, sb_rows[:ts], nb, ts)
        y, xlast, hlast = _mix_prompt(x, mix_norm[l, 0:1], w_in_b, l, y_s, nbatch, seq, cw, cb, wg, bg, lam,
                                      sgu_ln[l], sgu_w[l], sb_rows, tt)
        x = _matmul_norm_res(y, w_out_b, l, x, mix_norm[l, 1:2], tm, "mix_out")
        outs["cvp"].append(xlast[:, SUBLANES - (CONV_WIDTH - 1):, :])
        outs["hp"].append(hlast[:, 0, :])
        outs["cvs"].append(jnp.transpose(conv_s, (1, 0, 2)))
        outs["hs"].append(h_s)
        outs["vs"].append(jnp.transpose(v_s, (1, 0, 2)))

        q = _norm_matmul(x, xa_norm[l, 0:1], w_q_b, l, tm, d, BF16, "xa_q")
        qt = q[mp:].reshape(ts, ngroups, SAMPLE_GROUP, XA_HEADS, hd)
        qt = jnp.transpose(qt, (1, 3, 4, 2, 0)).reshape(ngroups, XA_HEADS, hd, SAMPLE_GROUP * ts)
        o_s = _attn_sample(qt, k_tiles, v_tiles, l, ts).astype(BF16)
        o_s = jnp.transpose(o_s.reshape(ngroups, SAMPLE_GROUP, ts, d), (2, 0, 1, 3)).reshape(ms, d)
        o = _attn_prompt(q, kt_p, v_p, l, o_s, nbatch, seq, tq)
        x = _matmul_norm_res(o, w_o_b, l, x, xa_norm[l, 1:2], tm, "xa_out")

        if l < depth - 1:
            nxt = [(a, l + 1) for a in (ffn1_w_in, ffn1_w_down, ffn2_w_in, ffn2_w_down)]
            x, *ffn_w = _ffn(x, ffn2_norm[l:l + 1], ffn2_wi, ffn2_wd, 0, tm, cast=nxt)
            ffn_w = [a[None] for a in ffn_w]

    last_norm = ffn2_norm[depth - 1:depth]
    y_prompt = _ffn(x, last_norm, ffn2_wi, ffn2_wd, 0, tm, rows=mp).reshape(nbatch, seq, d)
    y_sample = _ffn(x, last_norm, ffn2_wi, ffn2_wd, 0, tm, row_block0=mp // tm, rows=ms)
    y_sample = jnp.transpose(y_sample.reshape(ts, nb, d), (1, 0, 2))
    return (y_prompt, y_sample, _tiles_to_heads(ktile_p, mem, XA_HEADS), _tiles_to_heads(vtile_p, mem, XA_HEADS),
            jnp.stack(outs["cvp"]),
            jnp.stack(outs["hp"]), jnp.stack(outs["cvs"]), jnp.stack(outs["hs"]), jnp.stack(outs["vs"]))
```

```python
import functools
import math

import jax
import jax.numpy as jnp
from jax import lax
from jax.experimental import pallas as pl
from jax.experimental.pallas import tpu as pltpu

F32 = jnp.float32
BF16 = jnp.bfloat16

EPS = 1e-6
HALF = 0.5
LRU_C = 8.0
CONV_WIDTH = 4
CHUNK = 128
LRU_HEADS = 8
SGU_HEADS = 8
XA_HEADS = 4

VMEM_LIMIT_BYTES = 56 * 1024 * 1024
SUBLANES = 8
LANES = 128
BF16_ROWS = 16
FF_TILE = 1024
SAMPLE_GROUP = 8
CACHE_STEP_BATCH = 4
LRU_PAIR = 2 * LANES


def _params(*sem):
    return pltpu.CompilerParams(dimension_semantics=sem, vmem_limit_bytes=VMEM_LIMIT_BYTES)


def _pick_tile(n, prefs):
    for p in prefs:
        if n % p == 0:
            return p
    raise ValueError(f"no tile in {prefs} divides {n}")


def _rms(x, g):
    return (x * lax.rsqrt(jnp.mean(x * x, axis=-1, keepdims=True) + EPS)) * g


def _gelu(x):
    c = math.sqrt(2.0 / math.pi)
    return 0.5 * x * (1.0 + jnp.tanh(c * (x + 0.044715 * (x * x * x))))


def _softplus(x):
    return jnp.maximum(x, 0.0) + jnp.log1p(jnp.exp(-jnp.abs(x)))


def _ffn_kernel(x_ref, *refs, nf, last_cols, n_first, two_sources, ncast):
    i, f = pl.program_id(0), pl.program_id(1)
    refs = list(refs)
    xs_ref = refs.pop(0) if two_sources else None
    g_ref, wg_ref, wu_ref, wd_ref = refs[:4]
    cast_in, o_ref, cast_out, h_scr = refs[4:4 + ncast], refs[4 + ncast], refs[5 + ncast:5 + 2 * ncast], refs[-1]
    if two_sources:
        read_x = lambda: jnp.where(i < n_first, x_ref[...], xs_ref[...])
    else:
        read_x = lambda: x_ref[...]

    @pl.when(f == 0)
    def _():
        h_scr[...] = _rms(read_x(), g_ref[0:1, :]).astype(BF16)
        o_ref[...] = jnp.zeros_like(o_ref)

    def accumulate(cols):
        h = h_scr[...]
        gate = jnp.dot(h, wg_ref[:, :cols], preferred_element_type=F32)
        up = jnp.dot(h, wu_ref[0, :, FF_TILE - cols:], preferred_element_type=F32)
        act = ((gate * jax.nn.sigmoid(gate)) * up).astype(BF16)
        o_ref[...] += jnp.dot(act, wd_ref[:cols, :], preferred_element_type=F32)
        for src, dst in zip(cast_in, cast_out):
            dst[...] = src[...].astype(BF16)

    if last_cols == FF_TILE:
        accumulate(FF_TILE)
    else:
        @pl.when(f < nf - 1)
        def _():
            accumulate(FF_TILE)

        @pl.when(f == nf - 1)
        def _():
            accumulate(last_cols)

    @pl.when(f == nf - 1)
    def _():
        o_ref[...] = read_x() + HALF * _rms(o_ref[...], g_ref[1:2, :])


def _cast_blocking(shape, nsteps):
    r, c = shape
    ncb = 2 if c > 4096 and (c // 2) % LANES == 0 and nsteps >= 2 else 1
    br = -(-(-(-r // (nsteps // ncb))) // BF16_ROWS) * BF16_ROWS
    assert -(-r // br) * ncb <= nsteps
    return br, c // ncb, ncb


def _cast_specs(cast, nsteps, step_of):
    in_specs, out_specs = [], []
    for a, src_layer in cast:
        br, bc, ncb = _cast_blocking(a.shape[1:], nsteps)
        last = -(-a.shape[1] // br) * ncb - 1

        def blk(*idx, ncb=ncb, last=last):
            step = jnp.minimum(step_of(*idx), last)
            return step // ncb, step % ncb

        in_specs.append(pl.BlockSpec((None, br, bc), lambda *idx, blk=blk, sl=src_layer: (sl,) + blk(*idx)))
        out_specs.append(pl.BlockSpec((br, bc), blk))
    return in_specs, out_specs


def _ffn(x, norm, w_in, wd, layer, tm, xs=None, row_block0=0, rows=None, cast=()):
    d = x.shape[1]
    n_first = (x.shape[0] if rows is None else rows) // tm
    n_second = 0 if xs is None else xs.shape[0] // tm
    second = [] if xs is None else [(xs, pl.BlockSpec((tm, d), lambda i, f: (jnp.maximum(i - n_first, 0), 0)))]
    ff = wd.shape[1]
    assert ff % LANES == 0 and FF_TILE % LANES == 0 and ff >= FF_TILE
    nf = pl.cdiv(ff, FF_TILE)
    up_start = lambda f: jnp.minimum(ff // LANES + f * (FF_TILE // LANES), (2 * ff - FF_TILE) // LANES) * LANES
    cast_in_specs, cast_out_specs = _cast_specs(cast, (n_first + n_second) * nf, lambda i, f: i * nf + f)
    res = pl.pallas_call(
        functools.partial(_ffn_kernel, nf=nf, last_cols=ff - (nf - 1) * FF_TILE, n_first=n_first,
                          two_sources=xs is not None, ncast=len(cast)),
        grid=(n_first + n_second, nf),
        in_specs=[
            pl.BlockSpec((tm, d), lambda i, f: (jnp.minimum(i, n_first - 1) + row_block0, 0)),
            *[spec for _, spec in second],
            pl.BlockSpec((None, 2, d), lambda i, f: (layer, 0, 0)),
            pl.BlockSpec((None, d, FF_TILE), lambda i, f: (layer, 0, f)),
            pl.BlockSpec((pl.Element(1), pl.Element(d), pl.Element(FF_TILE)), lambda i, f: (layer, 0, up_start(f))),
            pl.BlockSpec((None, FF_TILE, d), lambda i, f: (layer, f, 0)),
            *cast_in_specs,
        ],
        out_specs=[pl.BlockSpec((tm, d), lambda i, f: (i, 0)), *cast_out_specs],
        out_shape=[jax.ShapeDtypeStruct(((n_first + n_second) * tm, d), F32),
                   *[jax.ShapeDtypeStruct(a.shape[1:], BF16) for a, _ in cast]],
        scratch_shapes=[pltpu.VMEM((tm, d), BF16)],
        compiler_params=_params("arbitrary", "arbitrary"),
        name="ffn",
    )(x, *[a for a, _ in second], norm, w_in, w_in, wd, *[a for a, _ in cast])
    return res if cast else res[0]


def _round_weight_once(w_ref, w_scr, first_step):
    @pl.when(first_step)
    def _():
        w_scr[...] = w_ref[...].astype(BF16)


def _norm_matmul_kernel(x_ref, g_ref, w_ref, o_ref, h_scr):
    @pl.when(pl.program_id(1) == 0)
    def _():
        h_scr[...] = _rms(x_ref[...], g_ref[...]).astype(BF16)

    o_ref[...] = jnp.dot(h_scr[...], w_ref[...], preferred_element_type=F32).astype(o_ref.dtype)


def _norm_matmul(x, g, w, layer, tm, tn, out_dtype, name, row_block0=0, rows=None):
    d = x.shape[1]
    m = x.shape[0] if rows is None else rows
    n = w.shape[2]
    return pl.pallas_call(
        _norm_matmul_kernel,
        grid=(m // tm, n // tn),
        in_specs=[
            pl.BlockSpec((tm, d), lambda i, j: (i + row_block0, 0)),
            pl.BlockSpec((1, d), lambda i, j: (0, 0)),
            pl.BlockSpec((None, d, tn), lambda i, j: (layer, 0, j)),
        ],
        out_specs=pl.BlockSpec((tm, tn), lambda i, j: (i, j)),
        out_shape=jax.ShapeDtypeStruct((m, n), out_dtype),
        scratch_shapes=[pltpu.VMEM((tm, d), BF16)],
        compiler_params=_params("parallel", "arbitrary"),
        name=name,
    )(x, g, w)


def _norm_matmul_f32w_kernel(x_ref, g_ref, w_ref, o_ref, w_scr):
    _round_weight_once(w_ref, w_scr, pl.program_id(0) == 0)
    h = _rms(x_ref[...], g_ref[...]).astype(BF16)
    o_ref[...] = jnp.dot(h, w_scr[...], preferred_element_type=F32).astype(o_ref.dtype)


def _norm_matmul_f32w(x, g, w, layer, tm, out_dtype, name):
    m, d = x.shape
    n = w.shape[2]
    return pl.pallas_call(
        _norm_matmul_f32w_kernel,
        grid=(m // tm,),
        in_specs=[
            pl.BlockSpec((tm, d), lambda i: (i, 0)),
            pl.BlockSpec((1, d), lambda i: (0, 0)),
            pl.BlockSpec((None, d, n), lambda i: (layer, 0, 0), pipeline_mode=pl.Buffered(1)),
        ],
        out_specs=pl.BlockSpec((tm, n), lambda i: (i, 0)),
        out_shape=jax.ShapeDtypeStruct((m, n), out_dtype),
        scratch_shapes=[pltpu.VMEM((d, n), BF16)],
        compiler_params=_params("arbitrary"),
        name=name,
    )(x, g, w)


def _matmul_norm_res_kernel(y_ref, w_ref, x_ref, g_ref, o_ref, w_scr):
    _round_weight_once(w_ref, w_scr, pl.program_id(0) == 0)
    d = jnp.dot(y_ref[...], w_scr[...], preferred_element_type=F32)
    o_ref[...] = x_ref[...] + _rms(d, g_ref[...])


def _matmul_norm_res(y, w, layer, x, g, tm, name):
    m, k = y.shape
    n = w.shape[2]
    return pl.pallas_call(
        _matmul_norm_res_kernel,
        grid=(m // tm,),
        in_specs=[
            pl.BlockSpec((tm, k), lambda i: (i, 0)),
            pl.BlockSpec((None, k, n), lambda i: (layer, 0, 0), pipeline_mode=pl.Buffered(1)),
            pl.BlockSpec((tm, n), lambda i: (i, 0)),
            pl.BlockSpec((1, n), lambda i: (0, 0)),
        ],
        out_specs=pl.BlockSpec((tm, n), lambda i: (i, 0)),
        out_shape=jax.ShapeDtypeStruct((m, n), F32),
        scratch_shapes=[pltpu.VMEM((k, n), BF16)],
        compiler_params=_params("arbitrary"),
        name=name,
    )(y, w, x, g)


def _lru_gates(xc, wg, bg, c):
    g = jnp.dot(xc.astype(BF16), wg, preferred_element_type=F32) + bg
    r = jax.nn.sigmoid(g[:, :LRU_PAIR])
    i = jax.nn.sigmoid(g[:, LRU_PAIR:])
    log_a = c * r
    a = jnp.exp(log_a)
    b = jnp.sqrt(-jnp.tanh(log_a) * (a * a + 1.0)) * (i * xc)
    return a, b


def _layer_norm(x, g, b):
    mu = jnp.mean(x, axis=-1, keepdims=True)
    xc = x - mu
    return (xc * lax.rsqrt(jnp.mean(xc * xc, axis=-1, keepdims=True) + EPS)) * g + b


def _mix_prompt_kernel(x_ref, g_ref, w_ref, cw_ref, cb_ref, wg_ref, bg_ref, lam_ref, ln_ref, sw_ref, sb_ref, ys_ref,
                       *refs, nt, n_prompt, tt, dl, ds, ncast):
    cast_in, (y_ref, xlast_ref, hlast_ref) = refs[:ncast], refs[ncast:ncast + 3]
    cast_out, (z_a, z_b, xprev_scr, hc_scr, xc_scr) = refs[ncast + 3:2 * ncast + 3], refs[2 * ncast + 3:]
    i = pl.program_id(0)
    t = (i + nt - 1) % nt

    @pl.when(i == 0)
    def _():
        z_b[...] = jnp.zeros_like(z_b)

    @pl.when((t == 0) | (i == 0))
    def _():
        xprev_scr[...] = jnp.zeros_like(xprev_scr)
        hc_scr[...] = jnp.zeros_like(hc_scr)

    def step(z_write, z_read):
        h = _rms(x_ref[...], g_ref[...]).astype(BF16)
        dz = w_ref.shape[1]

        def project(k, of):
            cols = slice(k * dz // of, (k + 1) * dz // of)
            z_write[:, cols] = jnp.dot(h, w_ref[:, cols], preferred_element_type=F32)

        _mixer_prompt_tile(z_read, cw_ref, cb_ref, wg_ref, bg_ref, lam_ref, ln_ref, sw_ref, sb_ref,
                           y_ref, xlast_ref, hlast_ref, xprev_scr, hc_scr, xc_scr, project, tt=tt, dl=dl, ds=ds)
        for src, dst in zip(cast_in, cast_out):
            dst[...] = src[...].astype(BF16)

    @pl.when((i <= n_prompt) & (i % 2 == 0))
    def _():
        step(z_a, z_b)

    @pl.when((i <= n_prompt) & (i % 2 == 1))
    def _():
        step(z_b, z_a)

    @pl.when(i > n_prompt)
    def _():
        y_ref[...] = ys_ref[...]


def _mixer_prompt_tile(z_ref, cw_ref, cb_ref, wg_ref, bg_ref, lam_ref, ln_ref, sw_ref, sb_ref,
                       y_ref, xlast_ref, hlast_ref, xprev_scr, hc_scr, xc_scr, project, *, tt, dl, ds):
    ng = tt // SUBLANES
    npairs = dl // LRU_PAIR
    nslices = 4 * npairs
    project(0, nslices)
    project(1, nslices)

    x = z_ref[:, 0:dl]
    w = cw_ref[...]
    cb = cb_ref[...]
    acc = cb + w[CONV_WIDTH - 1:CONV_WIDTH, :] * x
    for k in range(1, CONV_WIDTH):
        acc = acc + w[CONV_WIDTH - 1 - k:CONV_WIDTH - k, :] * pltpu.roll(x, k, 0)
    xc_scr[...] = acc
    x_head = x[0:SUBLANES, :]
    ext = jnp.concatenate([xprev_scr[...], x_head], axis=0)
    acc = cb + w[CONV_WIDTH - 1:CONV_WIDTH, :] * x_head
    for k in range(1, CONV_WIDTH):
        acc = acc + w[CONV_WIDTH - 1 - k:CONV_WIDTH - k, :] * pltpu.roll(ext, k, 0)[SUBLANES:, :]
    xc_scr[0:SUBLANES, :] = acc
    x_tail = x[tt - SUBLANES:tt, :]
    xprev_scr[...] = x_tail
    xlast_ref[0] = x_tail

    c_all = -LRU_C * _softplus(-lam_ref[...])
    sidx = lax.broadcasted_iota(jnp.int32, (ng, SUBLANES, LRU_PAIR), 1)
    for p in range(npairs):
        sl = slice(p * LRU_PAIR, (p + 1) * LRU_PAIR)
        project(2 + 2 * p, nslices)
        a, b = _lru_gates(xc_scr[:, sl], wg_ref[p], bg_ref[p], c_all[:, sl])
        a = a.reshape(ng, SUBLANES, LRU_PAIR)
        b = b.reshape(ng, SUBLANES, LRU_PAIR)
        shift = 1
        while shift < SUBLANES:
            keep = sidx >= shift
            b = jnp.where(keep, a * pltpu.roll(b, shift, 1) + b, b)
            a = jnp.where(keep, a * pltpu.roll(a, shift, 1), a)
            shift *= 2
        project(3 + 2 * p, nslices)
        carry = hc_scr[:, sl]
        groups = []
        for gi in range(ng):
            hg = a[gi] * carry + b[gi]
            groups.append(hg)
            carry = hg[SUBLANES - 1:SUBLANES, :]
        hc_scr[:, sl] = carry
        hlast_ref[0, :, sl] = carry
        h_seq = jnp.concatenate(groups, axis=0)
        y_ref[:, sl] = (h_seq * _gelu(z_ref[:, dl + p * LRU_PAIR:dl + (p + 1) * LRU_PAIR])).astype(BF16)

    project(2 * npairs + 2, nslices)
    project(2 * npairs + 3, nslices)
    v = _layer_norm(_gelu(z_ref[:, 2 * dl + ds:2 * dl + 2 * ds]), ln_ref[0:1, :], ln_ref[1:2, :])
    vb = v.astype(BF16)
    project(2 * npairs + 4, nslices)
    project(2 * npairs + 5, nslices)
    u = _gelu(z_ref[:, 2 * dl:2 * dl + ds])
    nc = tt // CHUNK
    hd = ds // SGU_HEADS
    row = lax.broadcasted_iota(jnp.int32, (CHUNK, CHUNK), 0)
    col = lax.broadcasted_iota(jnp.int32, (CHUNK, CHUNK), 1)
    for g in range(SGU_HEADS):
        if g == SGU_HEADS // 2:
            project(2 * npairs + 6, nslices)
            project(2 * npairs + 7, nslices)
        cs = slice(g * hd, (g + 1) * hd)
        wt = jnp.where(row >= col, sw_ref[g], 0.0).astype(BF16)
        vg = jnp.concatenate([vb[c * CHUNK:(c + 1) * CHUNK, cs] for c in range(nc)], axis=1)
        s = jnp.dot(wt, vg, preferred_element_type=F32)
        for c in range(nc):
            rs = slice(c * CHUNK, (c + 1) * CHUNK)
            sc = s[:, c * hd:(c + 1) * hd] + sb_ref[:, cs]
            y_ref[rs, dl + g * hd:dl + (g + 1) * hd] = (u[rs, cs] * sc).astype(BF16)
    assert 2 * npairs + 8 == nslices


def _mix_prompt(x, g, w_in, layer, ys, nbatch, seq, cw, cb, wg, bg, lam, ln, sw, sb, tt, cast=()):
    d = x.shape[1]
    dz = w_in.shape[2]
    dl = cw.shape[1]
    ds = ln.shape[1]
    nt = seq // tt
    n_prompt = nbatch * nt
    n_sample = ys.shape[0] // tt
    full = lambda *shape: pl.BlockSpec(shape, lambda i: (0,) * len(shape))
    batch_of = lambda i: jnp.clip((i - 1) // nt, 0, nbatch - 1)
    cast_in_specs, cast_out_specs = _cast_specs(cast, n_prompt + 1, lambda i: i)
    return pl.pallas_call(
        functools.partial(_mix_prompt_kernel, nt=nt, n_prompt=n_prompt, tt=tt, dl=dl, ds=ds, ncast=len(cast)),
        grid=(n_prompt + 1 + n_sample,),
        in_specs=[
            pl.BlockSpec((tt, d), lambda i: (jnp.minimum(i, n_prompt - 1), 0)),
            full(*g.shape),
            pl.BlockSpec((None, d, dz), lambda i: (layer, 0, 0), pipeline_mode=pl.Buffered(1)),
            full(*cw.shape), full(*cb.shape), full(*wg.shape), full(*bg.shape), full(*lam.shape),
            full(*ln.shape), full(*sw.shape), full(*sb.shape),
            pl.BlockSpec((tt, dl + ds), lambda i: (jnp.clip(i - n_prompt - 1, 0, n_sample - 1), 0)),
            *cast_in_specs,
        ],
        out_specs=[
            pl.BlockSpec((tt, dl + ds), lambda i: (jnp.maximum(i - 1, 0), 0)),
            pl.BlockSpec((1, SUBLANES, dl), lambda i: (batch_of(i), 0, 0)),
            pl.BlockSpec((1, 1, dl), lambda i: (batch_of(i), 0, 0)),
            *cast_out_specs,
        ],
        out_shape=[
            jax.ShapeDtypeStruct((nbatch * seq + ys.shape[0], dl + ds), BF16),
            jax.ShapeDtypeStruct((nbatch, SUBLANES, dl), F32),
            jax.ShapeDtypeStruct((nbatch, 1, dl), F32),
            *[jax.ShapeDtypeStruct(a.shape[1:], BF16) for a, _ in cast],
        ],
        scratch_shapes=[
            pltpu.VMEM((tt, dz), F32),
            pltpu.VMEM((tt, dz), F32),
            pltpu.VMEM((SUBLANES, dl), F32),
            pltpu.VMEM((1, dl), F32),
            pltpu.VMEM((tt, dl), F32),
        ],
        compiler_params=_params("arbitrary"),
        name="mix_prompt",
    )(x, g, w_in, cw, cb, wg, bg, lam, ln, sw, sb, ys, *[a for a, _ in cast])


def _mixer_sample_kernel(z_ref, cbuf_ref, h0_ref, cw_ref, cb_ref, wg_ref, bg_ref, lam_ref, ln_ref,
                         swx_ref, sbx_ref,
                         y_ref, convnew_ref, hlast_ref, v_ref, xc_scr, *, nb, ts, dl, ds):
    rows = lambda t: slice(t * nb, (t + 1) * nb)
    hist = CONV_WIDTH - 1
    xx = [cbuf_ref[k] for k in range(hist)] + [z_ref[rows(t), 0:dl] for t in range(ts)]
    w = cw_ref[...]
    cb = cb_ref[...]
    for t in range(ts):
        acc = cb + w[0:1, :] * xx[t]
        for k in range(1, CONV_WIDTH):
            acc = acc + w[k:k + 1, :] * xx[t + k]
        xc_scr[rows(t), :] = acc
    for k in range(hist):
        convnew_ref[k] = xx[ts + k]

    c_all = -LRU_C * _softplus(-lam_ref[...])
    for p in range(dl // LRU_PAIR):
        sl = slice(p * LRU_PAIR, (p + 1) * LRU_PAIR)
        a, b = _lru_gates(xc_scr[:, sl], wg_ref[p], bg_ref[p], c_all[:, sl])
        h = h0_ref[:, sl]
        for t in range(ts):
            h = a[rows(t), :] * h + b[rows(t), :]
            gl = z_ref[rows(t), dl + p * LRU_PAIR:dl + (p + 1) * LRU_PAIR]
            y_ref[rows(t), sl] = (h * _gelu(gl)).astype(BF16)
        hlast_ref[:, sl] = h

    vs = []
    for t in range(ts):
        v = _layer_norm(_gelu(z_ref[rows(t), 2 * dl + ds:2 * dl + 2 * ds]), ln_ref[0:1, :], ln_ref[1:2, :])
        v_ref[t] = v
        vs.append(v)
    for t in range(ts):
        s = sbx_ref[t:t + 1, :] + swx_ref[t * ts:t * ts + 1, :] * vs[0]
        for j in range(1, t + 1):
            s = s + swx_ref[t * ts + j:t * ts + j + 1, :] * vs[j]
        u = _gelu(z_ref[rows(t), 2 * dl:2 * dl + ds])
        y_ref[rows(t), dl:dl + ds] = (u * s).astype(BF16)


def _mixer_sample(z, cbuf, h0, cw, cb, wg, bg, lam, ln, swx, sbx, nb, ts):
    dl = cw.shape[1]
    ds = ln.shape[1]
    return pl.pallas_call(
        functools.partial(_mixer_sample_kernel, nb=nb, ts=ts, dl=dl, ds=ds),
        out_shape=[
            jax.ShapeDtypeStruct((ts * nb, dl + ds), BF16),
            jax.ShapeDtypeStruct((CONV_WIDTH - 1, nb, dl), F32),
            jax.ShapeDtypeStruct((nb, dl), F32),
            jax.ShapeDtypeStruct((ts, nb, ds), F32),
        ],
        scratch_shapes=[pltpu.VMEM((ts * nb, dl), F32)],
        compiler_params=pltpu.CompilerParams(vmem_limit_bytes=VMEM_LIMIT_BYTES),
        name="mixer_sample",
    )(z, cbuf, h0, cw, cb, wg, bg, lam, ln, swx, sbx)


def _mem_kv_kernel(x_ref, g_ref, w_ref, kt_ref, vb_ref, ktile_ref, vtile_ref, *, nh):
    mem = x_ref.shape[0]
    d = w_ref.shape[1] // 2
    hd = d // nh
    chunks = hd // LANES
    kv = jnp.dot(_rms(x_ref[...], g_ref[...]).astype(BF16), w_ref[...], preferred_element_type=F32)
    k, v = kv[:, :d], kv[:, d:]
    kt_ref[...] = k.T.astype(BF16)
    vb_ref[...] = v.astype(BF16)
    for tile_ref, val in ((ktile_ref, k), (vtile_ref, v)):
        for h in range(nh):
            for c in range(chunks):
                lanes = slice(h * hd + c * LANES, h * hd + (c + 1) * LANES)
                tile_ref[pl.ds(c * nh + h, mem, stride=nh * chunks), :] = val[:, lanes]


def _mem_kv(memf, g, w_kv, nbatch):
    depth, d, d2 = w_kv.shape
    mem = memf.shape[0] // nbatch
    rows = mem * d // LANES
    per_lb = lambda *blk: pl.BlockSpec((None, None) + blk, lambda l, b: (l, b, 0, 0))
    return pl.pallas_call(
        functools.partial(_mem_kv_kernel, nh=XA_HEADS),
        grid=(depth, nbatch),
        in_specs=[
            pl.BlockSpec((mem, d), lambda l, b: (b, 0)),
            pl.BlockSpec((None, 1, d), lambda l, b: (l, 0, 0)),
            pl.BlockSpec((None, d, d2), lambda l, b: (l, 0, 0)),
        ],
        out_specs=[per_lb(d, mem), per_lb(mem, d), per_lb(rows, LANES), per_lb(rows, LANES)],
        out_shape=[
            jax.ShapeDtypeStruct((depth, nbatch, d, mem), BF16),
            jax.ShapeDtypeStruct((depth, nbatch, mem, d), BF16),
            jax.ShapeDtypeStruct((depth, nbatch, rows, LANES), F32),
            jax.ShapeDtypeStruct((depth, nbatch, rows, LANES), F32),
        ],
        compiler_params=_params("arbitrary", "arbitrary"),
        name="mem_kv",
    )(memf, g, w_kv)


def _tiles_to_heads(t, mem, nh):
    depth, nbatch, rows, _ = t.shape
    chunks = rows // (mem * nh)
    t = t.reshape(depth, nbatch, mem, chunks, nh, LANES)
    return jnp.transpose(t, (0, 1, 2, 4, 3, 5)).reshape(depth, nbatch, mem, nh, chunks * LANES)


def _softmax_rows(s):
    e = jnp.exp(s - jnp.max(s, axis=-1, keepdims=True))
    return e / jnp.sum(e, axis=-1, keepdims=True)


def _attn_prompt_kernel(q_ref, kt_ref, v_ref, os_ref, o_ref, *, n_prompt, hd, scale):
    i = pl.program_id(0)

    @pl.when(i < n_prompt)
    def _():
        for h in range(XA_HEADS):
            hs = slice(h * hd, (h + 1) * hd)
            s = jnp.dot(q_ref[:, hs], kt_ref[0, hs, :], preferred_element_type=F32) * scale
            p = _softmax_rows(s).astype(BF16)
            o_ref[:, hs] = jnp.dot(p, v_ref[0, :, hs], preferred_element_type=F32).astype(BF16)

    @pl.when(i >= n_prompt)
    def _():
        o_ref[...] = os_ref[...]


def _attn_prompt(q, kt, v, layer, o_sample, nbatch, seq, tq):
    d = q.shape[1]
    mem = v.shape[2]
    hd = d // XA_HEADS
    nt = seq // tq
    n_prompt = nbatch * nt
    n_sample = o_sample.shape[0] // tq
    batch_of = lambda i: jnp.minimum(i // nt, nbatch - 1)
    return pl.pallas_call(
        functools.partial(_attn_prompt_kernel, n_prompt=n_prompt, hd=hd, scale=hd ** -0.5),
        grid=(n_prompt + n_sample,),
        in_specs=[
            pl.BlockSpec((tq, d), lambda i: (jnp.minimum(i, n_prompt - 1), 0)),
            pl.BlockSpec((None, 1, d, mem), lambda i: (layer, batch_of(i), 0, 0)),
            pl.BlockSpec((None, 1, mem, d), lambda i: (layer, batch_of(i), 0, 0)),
            pl.BlockSpec((tq, d), lambda i: (jnp.maximum(i - n_prompt, 0), 0)),
        ],
        out_specs=pl.BlockSpec((tq, d), lambda i: (i, 0)),
        out_shape=jax.ShapeDtypeStruct((nbatch * seq + o_sample.shape[0], d), BF16),
        compiler_params=_params("arbitrary"),
        name="attn_prompt",
    )(q, kt, v, o_sample)


def _attn_sample_kernel(qt_ref, k0_ref, k1_ref, v0_ref, v1_ref, o_ref, *, ts, scale, steps_per_group):
    gb, mem = k0_ref.shape[:2]
    nh, hd, ncol = qt_ref.shape[1:]
    sub = pl.program_id(0) % steps_per_group

    @pl.when(sub == 0)
    def _():
        o_ref[...] = jnp.zeros_like(o_ref)

    def head_rows(halves, h):
        flat = [r.reshape(gb, mem * SUBLANES, LANES) for r in halves]
        chunks = hd // LANES
        per_half = chunks // len(halves)
        return jnp.concatenate(
            [jnp.concatenate([flat[c // per_half][j, pl.ds((c % per_half) * nh + h, mem, stride=SUBLANES), :]
                              for c in range(chunks)], axis=1)
             for j in range(gb)], axis=0).astype(BF16)

    col_owner = lax.broadcasted_iota(jnp.int32, (gb, mem, ncol), 2) // ts
    own = col_owner == sub * gb + lax.broadcasted_iota(jnp.int32, (gb, mem, ncol), 0)
    for h in range(nh):
        st = jnp.dot(head_rows((k0_ref, k1_ref), h), qt_ref[0, h], preferred_element_type=F32) * scale
        st = st.reshape(gb, mem, ncol)
        e = jnp.exp(st - jnp.max(st, axis=1, keepdims=True))
        p = e / jnp.sum(e, axis=1, keepdims=True)
        pm = jnp.where(own, p, 0.0).astype(BF16).reshape(gb * mem, ncol)
        o = lax.dot_general(pm, head_rows((v0_ref, v1_ref), h), (((0,), (0,)), ((), ())),
                            preferred_element_type=F32)
        o_ref[0, :, h * hd:(h + 1) * hd] += o


def _attn_sample(qt, k_tiles, v_tiles, layer, ts):
    ngroups, nh, hd, ncol = qt.shape
    _, nb, mem, halves, _, _ = k_tiles.shape
    assert halves == 2 and nh * (hd // LANES) == halves * SUBLANES
    gb = CACHE_STEP_BATCH
    steps_per_group = nb // ngroups // gb

    def half_spec(half):
        return pl.BlockSpec((None, gb, mem, None, SUBLANES, LANES), lambda i: (layer, i, 0, half, 0, 0))

    return pl.pallas_call(
        functools.partial(_attn_sample_kernel, ts=ts, scale=hd ** -0.5, steps_per_group=steps_per_group),
        grid=(nb // gb,),
        in_specs=[
            pl.BlockSpec((1, nh, hd, ncol), lambda i: (i // steps_per_group, 0, 0, 0)),
            half_spec(0), half_spec(1), half_spec(0), half_spec(1),
        ],
        out_specs=pl.BlockSpec((1, ncol, nh * hd), lambda i: (i // steps_per_group, 0, 0)),
        out_shape=jax.ShapeDtypeStruct((ngroups, ncol, nh * hd), F32),
        compiler_params=_params("arbitrary"),
        name="attn_sample",
    )(qt, k_tiles, k_tiles, v_tiles, v_tiles)


def _cache_tiles(cache):
    depth, nb, mem, nh, hd = cache.shape
    c = cache.reshape(depth, nb, mem, nh, hd // LANES, LANES)
    return jnp.transpose(c, (0, 1, 2, 4, 3, 5)).reshape(depth, nb, mem, -1, SUBLANES, LANES)


def _pad_to(a, axis, size):
    pad = [(0, 0)] * a.ndim
    pad[axis] = (0, size - a.shape[axis])
    return jnp.pad(a, pad)


def _gate_weights(w_a, b_a, w_i, b_i):
    heads, hd, _ = w_a.shape
    per = LRU_PAIR // hd
    npair = heads // per

    def blockdiag(w):
        w = w.reshape(npair, per, hd, hd)
        eye = jnp.eye(per, dtype=w.dtype)
        return jnp.einsum('pade,ab->padbe', w, eye).reshape(npair, per * hd, per * hd)

    wg = jnp.concatenate([blockdiag(w_a), blockdiag(w_i)], axis=-1).astype(BF16)
    bg = jnp.concatenate([b_a.reshape(npair, 1, per * hd), b_i.reshape(npair, 1, per * hd)], axis=-1)
    return wg, bg


def kernel(x_prompt, x_sample, mem_prompt, cache_mem_k, cache_mem_v, state_conv, state_lru_h, ffn1_norm, ffn1_w_in, ffn1_w_down, mix_norm, w_in, conv_w, conv_b, lru_w_a, lru_b_a, lru_w_i, lru_b_i, lru_lambda, sgu_ln, sgu_w, sgu_b, w_out, xa_norm, xa_w_q, xa_w_kv, xa_w_o, ffn2_norm, ffn2_w_in, ffn2_w_down):
    nbatch, seq, d = x_prompt.shape
    nb, ts, _ = x_sample.shape
    depth = ffn1_norm.shape[0]
    mem = mem_prompt.shape[1]
    dl = conv_w.shape[-1]
    ds = sgu_ln.shape[-1]
    hd = d // XA_HEADS
    d_ff = ffn1_w_down.shape[1]
    mp, ms = nbatch * seq, nb * ts
    assert seq % CHUNK == 0 and ts < CHUNK and ts >= CONV_WIDTH - 1
    assert nb % SAMPLE_GROUP == 0 and SAMPLE_GROUP * ts <= LANES
    assert dl % LRU_PAIR == 0 and ds % SGU_HEADS == 0

    tm = _pick_tile(math.gcd(mp, ms), (512, 256, 128, 64))
    tt = _pick_tile(math.gcd(seq, ms), (256, 128))
    tq = _pick_tile(math.gcd(seq, ms), (512, 256, 128, 64))
    ngroups = nb // SAMPLE_GROUP

    x_p, x_s = x_prompt.reshape(mp, d), jnp.transpose(x_sample, (1, 0, 2)).reshape(ms, d)

    ffn_w = [ffn1_w_in[0:1].astype(BF16), ffn1_w_down[0:1].astype(BF16), None, None]
    w_in_b, w_kv_b = w_in.astype(BF16), xa_w_kv.astype(BF16)
    k_tiles, v_tiles = _cache_tiles(cache_mem_k), _cache_tiles(cache_mem_v)

    kt_p, v_p, ktile_p, vtile_p = _mem_kv(mem_prompt.reshape(nbatch * mem, d), xa_norm[:, 2:3], w_kv_b, nbatch)

    outs = {k: [] for k in ("cvp", "hp", "cvs", "hs", "vs")}
    for l in range(depth):
        ffn1_wi, ffn1_wd = ffn_w[:2]
        if l == 0:
            x = _ffn(x_p, ffn1_norm[l:l + 1], ffn1_wi, ffn1_wd, 0, tm, xs=x_s)
        else:
            x = _ffn(x, ffn1_norm[l:l + 1], ffn1_wi, ffn1_wd, 0, tm)

        wg, bg = _gate_weights(lru_w_a[l], lru_b_a[l], lru_w_i[l], lru_b_i[l])
        cw, cb, lam = conv_w[l], conv_b[l][None, :], lru_lambda[l][None, :]
        sb_rows = jnp.repeat(sgu_b[l].T, ds // SGU_HEADS, axis=1)
        tri = jnp.tril(jnp.ones((ts, ts), F32))
        swx = jnp.repeat(jnp.transpose(sgu_w[l][:, :ts, :ts] * tri, (1, 2, 0)).reshape(ts * ts, SGU_HEADS),
                         ds // SGU_HEADS, axis=1)
        z_s = _norm_matmul(x, mix_norm[l, 0:1], w_in_b, l, tm, _pick_tile(w_in.shape[2], (2048, 1024)), F32,
                           "mix_in_sample", row_block0=mp // tm, rows=ms)
        y_s, conv_s, h_s, v_s = _mixer_sample(
            z_s, jnp.transpose(state_conv[l], (1, 0, 2)), state_lru_h[l], cw, cb, wg, bg, lam, sgu_ln[l],
            swx, sb_rows[:ts], nb, ts)
        y, xlast, hlast, *cast_w = _mix_prompt(
            x, mix_norm[l, 0:1], w_in_b, l, y_s, nbatch, seq, cw, cb, wg, bg, lam, sgu_ln[l], sgu_w[l], sb_rows, tt,
            cast=[(ffn2_w_in, 0), (ffn2_w_down, 0)] if l == 0 else ())
        if l == 0:
            ffn_w[2:] = [a[None] for a in cast_w]
        x = _matmul_norm_res(y, w_out, l, x, mix_norm[l, 1:2], tm, "mix_out")
        outs["cvp"].append(xlast[:, SUBLANES - (CONV_WIDTH - 1):, :])
        outs["hp"].append(hlast[:, 0, :])
        outs["cvs"].append(jnp.transpose(conv_s, (1, 0, 2)))
        outs["hs"].append(h_s)
        outs["vs"].append(jnp.transpose(v_s, (1, 0, 2)))

        q = _norm_matmul_f32w(x, xa_norm[l, 0:1], xa_w_q, l, tm, BF16, "xa_q")
        qt = q[mp:].reshape(ts, ngroups, SAMPLE_GROUP, XA_HEADS, hd)
        qt = jnp.transpose(qt, (1, 3, 4, 2, 0)).reshape(ngroups, XA_HEADS, hd, SAMPLE_GROUP * ts)
        o_s = _attn_sample(qt, k_tiles, v_tiles, l, ts).astype(BF16)
        o_s = jnp.transpose(o_s.reshape(ngroups, SAMPLE_GROUP, ts, d), (2, 0, 1, 3)).reshape(ms, d)
        o = _attn_prompt(q, kt_p, v_p, l, o_s, nbatch, seq, tq)
        x = _matmul_norm_res(o, xa_w_o, l, x, xa_norm[l, 1:2], tm, "xa_out")

        ffn2_wi, ffn2_wd = ffn_w[2:]
        if l < depth - 1:
            nxt = [(a, l + 1) for a in (ffn1_w_in, ffn1_w_down, ffn2_w_in, ffn2_w_down)]
            x, *ffn_w = _ffn(x, ffn2_norm[l:l + 1], ffn2_wi, ffn2_wd, 0, tm, cast=nxt)
            ffn_w = [a[None] for a in ffn_w]

    last_norm = ffn2_norm[depth - 1:depth]
    y_prompt = _ffn(x, last_norm, ffn2_wi, ffn2_wd, 0, tm, rows=mp).reshape(nbatch, seq, d)
    y_sample = _ffn(x, last_norm, ffn2_wi, ffn2_wd, 0, tm, row_block0=mp // tm, rows=ms)
    y_sample = jnp.transpose(y_sample.reshape(ts, nb, d), (1, 0, 2))
    return (y_prompt, y_sample, _tiles_to_heads(ktile_p, mem, XA_HEADS), _tiles_to_heads(vtile_p, mem, XA_HEADS),
            jnp.stack(outs["cvp"]),
            jnp.stack(outs["hp"]), jnp.stack(outs["cvs"]), jnp.stack(outs["hs"]), jnp.stack(outs["vs"]))
```

```python
import functools
import math

import jax
import jax.numpy as jnp
from jax import lax
from jax.experimental import pallas as pl
from jax.experimental.pallas import tpu as pltpu

F32 = jnp.float32
BF16 = jnp.bfloat16

EPS = 1e-6
HALF = 0.5
LRU_C = 8.0
CONV_WIDTH = 4
CHUNK = 128
LRU_HEADS = 8
SGU_HEADS = 8
XA_HEADS = 4

VMEM_LIMIT_BYTES = 56 * 1024 * 1024
SUBLANES = 8
LANES = 128
BF16_ROWS = 16
FF_TILE = 1024
SAMPLE_GROUP = 8
CACHE_STEP_BATCH = 4
LRU_PAIR = 2 * LANES


def _params(*sem):
    return pltpu.CompilerParams(dimension_semantics=sem, vmem_limit_bytes=VMEM_LIMIT_BYTES)


def _pick_tile(n, prefs):
    for p in prefs:
        if n % p == 0:
            return p
    raise ValueError(f"no tile in {prefs} divides {n}")


def _rms(x, g):
    return (x * lax.rsqrt(jnp.mean(x * x, axis=-1, keepdims=True) + EPS)) * g


def _gelu(x):
    c = math.sqrt(2.0 / math.pi)
    return 0.5 * x * (1.0 + jnp.tanh(c * (x + 0.044715 * (x * x * x))))


def _softplus(x):
    return jnp.maximum(x, 0.0) + jnp.log1p(jnp.exp(-jnp.abs(x)))


def _ffn_kernel(x_ref, *refs, nf, last_cols, n_first, two_sources, ncast):
    i, f = pl.program_id(0), pl.program_id(1)
    refs = list(refs)
    xs_ref = refs.pop(0) if two_sources else None
    g_ref, wg_ref, wu_ref, wd_ref = refs[:4]
    cast_in, o_ref, cast_out, h_scr = refs[4:4 + ncast], refs[4 + ncast], refs[5 + ncast:5 + 2 * ncast], refs[-1]
    if two_sources:
        read_x = lambda: jnp.where(i < n_first, x_ref[...], xs_ref[...])
    else:
        read_x = lambda: x_ref[...]

    @pl.when(f == 0)
    def _():
        h_scr[...] = _rms(read_x(), g_ref[0:1, :]).astype(BF16)
        o_ref[...] = jnp.zeros_like(o_ref)

    def accumulate(cols):
        h = h_scr[...]
        gate = jnp.dot(h, wg_ref[:, :cols], preferred_element_type=F32)
        up = jnp.dot(h, wu_ref[0, :, FF_TILE - cols:], preferred_element_type=F32)
        act = ((gate * jax.nn.sigmoid(gate)) * up).astype(BF16)
        o_ref[...] += jnp.dot(act, wd_ref[:cols, :], preferred_element_type=F32)
        for src, dst in zip(cast_in, cast_out):
            dst[...] = src[...].astype(BF16)

    if last_cols == FF_TILE:
        accumulate(FF_TILE)
    else:
        @pl.when(f < nf - 1)
        def _():
            accumulate(FF_TILE)

        @pl.when(f == nf - 1)
        def _():
            accumulate(last_cols)

    @pl.when(f == nf - 1)
    def _():
        o_ref[...] = read_x() + HALF * _rms(o_ref[...], g_ref[1:2, :])


def _cast_blocking(shape, nsteps):
    r, c = shape
    ncb = 2 if c > 4096 and (c // 2) % LANES == 0 and nsteps >= 2 else 1
    br = -(-(-(-r // (nsteps // ncb))) // BF16_ROWS) * BF16_ROWS
    assert -(-r // br) * ncb <= nsteps
    return br, c // ncb, ncb


def _cast_specs(cast, nsteps, step_of):
    in_specs, out_specs = [], []
    for a, src_layer in cast:
        br, bc, ncb = _cast_blocking(a.shape[1:], nsteps)
        last = -(-a.shape[1] // br) * ncb - 1

        def blk(*idx, ncb=ncb, last=last):
            step = jnp.minimum(step_of(*idx), last)
            return step // ncb, step % ncb

        in_specs.append(pl.BlockSpec((None, br, bc), lambda *idx, blk=blk, sl=src_layer: (sl,) + blk(*idx)))
        out_specs.append(pl.BlockSpec((br, bc), blk))
    return in_specs, out_specs


def _ffn(x, norm, w_in, wd, layer, tm, xs=None, row_block0=0, rows=None, cast=()):
    d = x.shape[1]
    n_first = (x.shape[0] if rows is None else rows) // tm
    n_second = 0 if xs is None else xs.shape[0] // tm
    second = [] if xs is None else [(xs, pl.BlockSpec((tm, d), lambda i, f: (jnp.maximum(i - n_first, 0), 0)))]
    ff = wd.shape[1]
    assert ff % LANES == 0 and FF_TILE % LANES == 0 and ff >= FF_TILE
    nf = pl.cdiv(ff, FF_TILE)
    up_start = lambda f: jnp.minimum(ff // LANES + f * (FF_TILE // LANES), (2 * ff - FF_TILE) // LANES) * LANES
    cast_in_specs, cast_out_specs = _cast_specs(cast, (n_first + n_second) * nf, lambda i, f: i * nf + f)
    res = pl.pallas_call(
        functools.partial(_ffn_kernel, nf=nf, last_cols=ff - (nf - 1) * FF_TILE, n_first=n_first,
                          two_sources=xs is not None, ncast=len(cast)),
        grid=(n_first + n_second, nf),
        in_specs=[
            pl.BlockSpec((tm, d), lambda i, f: (jnp.minimum(i, n_first - 1) + row_block0, 0)),
            *[spec for _, spec in second],
            pl.BlockSpec((None, 2, d), lambda i, f: (layer, 0, 0)),
            pl.BlockSpec((None, d, FF_TILE), lambda i, f: (layer, 0, f)),
            pl.BlockSpec((pl.Element(1), pl.Element(d), pl.Element(FF_TILE)), lambda i, f: (layer, 0, up_start(f))),
            pl.BlockSpec((None, FF_TILE, d), lambda i, f: (layer, f, 0)),
            *cast_in_specs,
        ],
        out_specs=[pl.BlockSpec((tm, d), lambda i, f: (i, 0)), *cast_out_specs],
        out_shape=[jax.ShapeDtypeStruct(((n_first + n_second) * tm, d), F32),
                   *[jax.ShapeDtypeStruct(a.shape[1:], BF16) for a, _ in cast]],
        scratch_shapes=[pltpu.VMEM((tm, d), BF16)],
        compiler_params=_params("arbitrary", "arbitrary"),
        name="ffn",
    )(x, *[a for a, _ in second], norm, w_in, w_in, wd, *[a for a, _ in cast])
    return res if cast else res[0]


def _round_weight_once(w_ref, w_scr, first_step):
    @pl.when(first_step)
    def _():
        w_scr[...] = w_ref[...].astype(BF16)


def _norm_matmul_kernel(x_ref, g_ref, w_ref, o_ref, h_scr):
    @pl.when(pl.program_id(1) == 0)
    def _():
        h_scr[...] = _rms(x_ref[...], g_ref[...]).astype(BF16)

    o_ref[...] = jnp.dot(h_scr[...], w_ref[...], preferred_element_type=F32).astype(o_ref.dtype)


def _norm_matmul(x, g, w, layer, tm, tn, out_dtype, name, row_block0=0, rows=None):
    d = x.shape[1]
    m = x.shape[0] if rows is None else rows
    n = w.shape[2]
    return pl.pallas_call(
        _norm_matmul_kernel,
        grid=(m // tm, n // tn),
        in_specs=[
            pl.BlockSpec((tm, d), lambda i, j: (i + row_block0, 0)),
            pl.BlockSpec((1, d), lambda i, j: (0, 0)),
            pl.BlockSpec((None, d, tn), lambda i, j: (layer, 0, j)),
        ],
        out_specs=pl.BlockSpec((tm, tn), lambda i, j: (i, j)),
        out_shape=jax.ShapeDtypeStruct((m, n), out_dtype),
        scratch_shapes=[pltpu.VMEM((tm, d), BF16)],
        compiler_params=_params("parallel", "arbitrary"),
        name=name,
    )(x, g, w)


def _norm_matmul_f32w_kernel(x_ref, g_ref, w_ref, o_ref, w_scr):
    _round_weight_once(w_ref, w_scr, pl.program_id(0) == 0)
    h = _rms(x_ref[...], g_ref[...]).astype(BF16)
    o_ref[...] = jnp.dot(h, w_scr[...], preferred_element_type=F32).astype(o_ref.dtype)


def _norm_matmul_f32w(x, g, w, layer, tm, out_dtype, name):
    m, d = x.shape
    n = w.shape[2]
    return pl.pallas_call(
        _norm_matmul_f32w_kernel,
        grid=(m // tm,),
        in_specs=[
            pl.BlockSpec((tm, d), lambda i: (i, 0)),
            pl.BlockSpec((1, d), lambda i: (0, 0)),
            pl.BlockSpec((None, d, n), lambda i: (layer, 0, 0), pipeline_mode=pl.Buffered(1)),
        ],
        out_specs=pl.BlockSpec((tm, n), lambda i: (i, 0)),
        out_shape=jax.ShapeDtypeStruct((m, n), out_dtype),
        scratch_shapes=[pltpu.VMEM((d, n), BF16)],
        compiler_params=_params("arbitrary"),
        name=name,
    )(x, g, w)


def _matmul_norm_res_kernel(y_ref, w_ref, x_ref, g_ref, o_ref, w_scr):
    _round_weight_once(w_ref, w_scr, pl.program_id(0) == 0)
    d = jnp.dot(y_ref[...], w_scr[...], preferred_element_type=F32)
    o_ref[...] = x_ref[...] + _rms(d, g_ref[...])


def _matmul_norm_res(y, w, layer, x, g, tm, name):
    m, k = y.shape
    n = w.shape[2]
    return pl.pallas_call(
        _matmul_norm_res_kernel,
        grid=(m // tm,),
        in_specs=[
            pl.BlockSpec((tm, k), lambda i: (i, 0)),
            pl.BlockSpec((None, k, n), lambda i: (layer, 0, 0), pipeline_mode=pl.Buffered(1)),
            pl.BlockSpec((tm, n), lambda i: (i, 0)),
            pl.BlockSpec((1, n), lambda i: (0, 0)),
        ],
        out_specs=pl.BlockSpec((tm, n), lambda i: (i, 0)),
        out_shape=jax.ShapeDtypeStruct((m, n), F32),
        scratch_shapes=[pltpu.VMEM((k, n), BF16)],
        compiler_params=_params("arbitrary"),
        name=name,
    )(y, w, x, g)


def _lru_gates(xc, wg, bg, c):
    g = jnp.dot(xc.astype(BF16), wg, preferred_element_type=F32) + bg
    r = jax.nn.sigmoid(g[:, :LRU_PAIR])
    i = jax.nn.sigmoid(g[:, LRU_PAIR:])
    log_a = c * r
    a = jnp.exp(log_a)
    b = jnp.sqrt(-jnp.tanh(log_a) * (a * a + 1.0)) * (i * xc)
    return a, b


def _layer_norm(x, g, b):
    mu = jnp.mean(x, axis=-1, keepdims=True)
    xc = x - mu
    return (xc * lax.rsqrt(jnp.mean(xc * xc, axis=-1, keepdims=True) + EPS)) * g + b


def _mix_prompt_kernel(x_ref, g_ref, w_ref, cw_ref, cb_ref, wg_ref, bg_ref, lam_ref, ln_ref, sw_ref, sb_ref, ys_ref,
                       *refs, nt, n_prompt, tt, dl, ds, ncast):
    cast_in, (y_ref, xlast_ref, hlast_ref) = refs[:ncast], refs[ncast:ncast + 3]
    cast_out, (z_a, z_b, xprev_scr, hc_scr, xc_scr) = refs[ncast + 3:2 * ncast + 3], refs[2 * ncast + 3:]
    i = pl.program_id(0)
    t = (i + nt - 1) % nt

    @pl.when(i == 0)
    def _():
        z_b[...] = jnp.zeros_like(z_b)

    @pl.when((t == 0) | (i == 0))
    def _():
        xprev_scr[...] = jnp.zeros_like(xprev_scr)
        hc_scr[...] = jnp.zeros_like(hc_scr)

    def step(z_write, z_read):
        h = _rms(x_ref[...], g_ref[...]).astype(BF16)
        dz = w_ref.shape[1]

        def project(k, of):
            cols = slice(k * dz // of, (k + 1) * dz // of)
            z_write[:, cols] = jnp.dot(h, w_ref[:, cols], preferred_element_type=F32)

        _mixer_prompt_tile(z_read, cw_ref, cb_ref, wg_ref, bg_ref, lam_ref, ln_ref, sw_ref, sb_ref,
                           y_ref, xlast_ref, hlast_ref, xprev_scr, hc_scr, xc_scr, project, tt=tt, dl=dl, ds=ds)
        for src, dst in zip(cast_in, cast_out):
            dst[...] = src[...].astype(BF16)

    @pl.when((i <= n_prompt) & (i % 2 == 0))
    def _():
        step(z_a, z_b)

    @pl.when((i <= n_prompt) & (i % 2 == 1))
    def _():
        step(z_b, z_a)

    @pl.when(i > n_prompt)
    def _():
        y_ref[...] = ys_ref[...]


def _mixer_prompt_tile(z_ref, cw_ref, cb_ref, wg_ref, bg_ref, lam_ref, ln_ref, sw_ref, sb_ref,
                       y_ref, xlast_ref, hlast_ref, xprev_scr, hc_scr, xc_scr, project, *, tt, dl, ds):
    ng = tt // SUBLANES
    npairs = dl // LRU_PAIR
    nslices = 4 * npairs
    project(0, nslices)
    project(1, nslices)

    x = z_ref[:, 0:dl]
    w = cw_ref[...]
    cb = cb_ref[...]
    acc = cb + w[CONV_WIDTH - 1:CONV_WIDTH, :] * x
    for k in range(1, CONV_WIDTH):
        acc = acc + w[CONV_WIDTH - 1 - k:CONV_WIDTH - k, :] * pltpu.roll(x, k, 0)
    xc_scr[...] = acc
    x_head = x[0:SUBLANES, :]
    ext = jnp.concatenate([xprev_scr[...], x_head], axis=0)
    acc = cb + w[CONV_WIDTH - 1:CONV_WIDTH, :] * x_head
    for k in range(1, CONV_WIDTH):
        acc = acc + w[CONV_WIDTH - 1 - k:CONV_WIDTH - k, :] * pltpu.roll(ext, k, 0)[SUBLANES:, :]
    xc_scr[0:SUBLANES, :] = acc
    x_tail = x[tt - SUBLANES:tt, :]
    xprev_scr[...] = x_tail
    xlast_ref[0] = x_tail

    c_all = -LRU_C * _softplus(-lam_ref[...])
    sidx = lax.broadcasted_iota(jnp.int32, (ng, SUBLANES, LRU_PAIR), 1)
    for p in range(npairs):
        sl = slice(p * LRU_PAIR, (p + 1) * LRU_PAIR)
        project(2 + 2 * p, nslices)
        a, b = _lru_gates(xc_scr[:, sl], wg_ref[p], bg_ref[p], c_all[:, sl])
        a = a.reshape(ng, SUBLANES, LRU_PAIR)
        b = b.reshape(ng, SUBLANES, LRU_PAIR)
        shift = 1
        while shift < SUBLANES:
            keep = sidx >= shift
            b = jnp.where(keep, a * pltpu.roll(b, shift, 1) + b, b)
            a = jnp.where(keep, a * pltpu.roll(a, shift, 1), a)
            shift *= 2
        project(3 + 2 * p, nslices)
        carry = hc_scr[:, sl]
        groups = []
        for gi in range(ng):
            hg = a[gi] * carry + b[gi]
            groups.append(hg)
            carry = hg[SUBLANES - 1:SUBLANES, :]
        hc_scr[:, sl] = carry
        hlast_ref[0, :, sl] = carry
        h_seq = jnp.concatenate(groups, axis=0)
        y_ref[:, sl] = (h_seq * _gelu(z_ref[:, dl + p * LRU_PAIR:dl + (p + 1) * LRU_PAIR])).astype(BF16)

    project(2 * npairs + 2, nslices)
    project(2 * npairs + 3, nslices)
    v = _layer_norm(_gelu(z_ref[:, 2 * dl + ds:2 * dl + 2 * ds]), ln_ref[0:1, :], ln_ref[1:2, :])
    vb = v.astype(BF16)
    project(2 * npairs + 4, nslices)
    project(2 * npairs + 5, nslices)
    u = _gelu(z_ref[:, 2 * dl:2 * dl + ds])
    nc = tt // CHUNK
    hd = ds // SGU_HEADS
    row = lax.broadcasted_iota(jnp.int32, (CHUNK, CHUNK), 0)
    col = lax.broadcasted_iota(jnp.int32, (CHUNK, CHUNK), 1)
    for g in range(SGU_HEADS):
        if g == SGU_HEADS // 2:
            project(2 * npairs + 6, nslices)
            project(2 * npairs + 7, nslices)
        cs = slice(g * hd, (g + 1) * hd)
        wt = jnp.where(row >= col, sw_ref[g], 0.0).astype(BF16)
        vg = jnp.concatenate([vb[c * CHUNK:(c + 1) * CHUNK, cs] for c in range(nc)], axis=1)
        s = jnp.dot(wt, vg, preferred_element_type=F32)
        for c in range(nc):
            rs = slice(c * CHUNK, (c + 1) * CHUNK)
            sc = s[:, c * hd:(c + 1) * hd] + sb_ref[:, cs]
            y_ref[rs, dl + g * hd:dl + (g + 1) * hd] = (u[rs, cs] * sc).astype(BF16)
    assert 2 * npairs + 8 == nslices


def _mix_prompt(x, g, w_in, layer, ys, nbatch, seq, cw, cb, wg, bg, lam, ln, sw, sb, tt, cast=()):
    d = x.shape[1]
    dz = w_in.shape[2]
    dl = cw.shape[1]
    ds = ln.shape[1]
    nt = seq // tt
    n_prompt = nbatch * nt
    n_sample = ys.shape[0] // tt
    full = lambda *shape: pl.BlockSpec(shape, lambda i: (0,) * len(shape))
    batch_of = lambda i: jnp.clip((i - 1) // nt, 0, nbatch - 1)
    cast_in_specs, cast_out_specs = _cast_specs(cast, n_prompt + 1, lambda i: i)
    return pl.pallas_call(
        functools.partial(_mix_prompt_kernel, nt=nt, n_prompt=n_prompt, tt=tt, dl=dl, ds=ds, ncast=len(cast)),
        grid=(n_prompt + 1 + n_sample,),
        in_specs=[
            pl.BlockSpec((tt, d), lambda i: (jnp.minimum(i, n_prompt - 1), 0)),
            full(*g.shape),
            pl.BlockSpec((None, d, dz), lambda i: (layer, 0, 0), pipeline_mode=pl.Buffered(1)),
            full(*cw.shape), full(*cb.shape), full(*wg.shape), full(*bg.shape), full(*lam.shape),
            full(*ln.shape), full(*sw.shape), full(*sb.shape),
            pl.BlockSpec((tt, dl + ds), lambda i: (jnp.clip(i - n_prompt - 1, 0, n_sample - 1), 0)),
            *cast_in_specs,
        ],
        out_specs=[
            pl.BlockSpec((tt, dl + ds), lambda i: (jnp.maximum(i - 1, 0), 0)),
            pl.BlockSpec((1, SUBLANES, dl), lambda i: (batch_of(i), 0, 0)),
            pl.BlockSpec((1, 1, dl), lambda i: (batch_of(i), 0, 0)),
            *cast_out_specs,
        ],
        out_shape=[
            jax.ShapeDtypeStruct((nbatch * seq + ys.shape[0], dl + ds), BF16),
            jax.ShapeDtypeStruct((nbatch, SUBLANES, dl), F32),
            jax.ShapeDtypeStruct((nbatch, 1, dl), F32),
            *[jax.ShapeDtypeStruct(a.shape[1:], BF16) for a, _ in cast],
        ],
        scratch_shapes=[
            pltpu.VMEM((tt, dz), F32),
            pltpu.VMEM((tt, dz), F32),
            pltpu.VMEM((SUBLANES, dl), F32),
            pltpu.VMEM((1, dl), F32),
            pltpu.VMEM((tt, dl), F32),
        ],
        compiler_params=_params("arbitrary"),
        name="mix_prompt",
    )(x, g, w_in, cw, cb, wg, bg, lam, ln, sw, sb, ys, *[a for a, _ in cast])


def _mixer_sample_kernel(z_ref, cbuf_ref, h0_ref, cw_ref, cb_ref, wg_ref, bg_ref, lam_ref, ln_ref,
                         swx_ref, sbx_ref,
                         y_ref, convnew_ref, hlast_ref, v_ref, xc_scr, *, nb, ts, dl, ds):
    rows = lambda t: slice(t * nb, (t + 1) * nb)
    hist = CONV_WIDTH - 1
    xx = [cbuf_ref[k] for k in range(hist)] + [z_ref[rows(t), 0:dl] for t in range(ts)]
    w = cw_ref[...]
    cb = cb_ref[...]
    for t in range(ts):
        acc = cb + w[0:1, :] * xx[t]
        for k in range(1, CONV_WIDTH):
            acc = acc + w[k:k + 1, :] * xx[t + k]
        xc_scr[rows(t), :] = acc
    for k in range(hist):
        convnew_ref[k] = xx[ts + k]

    c_all = -LRU_C * _softplus(-lam_ref[...])
    for p in range(dl // LRU_PAIR):
        sl = slice(p * LRU_PAIR, (p + 1) * LRU_PAIR)
        a, b = _lru_gates(xc_scr[:, sl], wg_ref[p], bg_ref[p], c_all[:, sl])
        h = h0_ref[:, sl]
        for t in range(ts):
            h = a[rows(t), :] * h + b[rows(t), :]
            gl = z_ref[rows(t), dl + p * LRU_PAIR:dl + (p + 1) * LRU_PAIR]
            y_ref[rows(t), sl] = (h * _gelu(gl)).astype(BF16)
        hlast_ref[:, sl] = h

    vs = []
    for t in range(ts):
        v = _layer_norm(_gelu(z_ref[rows(t), 2 * dl + ds:2 * dl + 2 * ds]), ln_ref[0:1, :], ln_ref[1:2, :])
        v_ref[t] = v
        vs.append(v)
    for t in range(ts):
        s = sbx_ref[t:t + 1, :] + swx_ref[t * ts:t * ts + 1, :] * vs[0]
        for j in range(1, t + 1):
            s = s + swx_ref[t * ts + j:t * ts + j + 1, :] * vs[j]
        u = _gelu(z_ref[rows(t), 2 * dl:2 * dl + ds])
        y_ref[rows(t), dl:dl + ds] = (u * s).astype(BF16)


def _mixer_sample(z, cbuf, h0, cw, cb, wg, bg, lam, ln, swx, sbx, nb, ts):
    dl = cw.shape[1]
    ds = ln.shape[1]
    return pl.pallas_call(
        functools.partial(_mixer_sample_kernel, nb=nb, ts=ts, dl=dl, ds=ds),
        out_shape=[
            jax.ShapeDtypeStruct((ts * nb, dl + ds), BF16),
            jax.ShapeDtypeStruct((CONV_WIDTH - 1, nb, dl), F32),
            jax.ShapeDtypeStruct((nb, dl), F32),
            jax.ShapeDtypeStruct((ts, nb, ds), F32),
        ],
        scratch_shapes=[pltpu.VMEM((ts * nb, dl), F32)],
        compiler_params=pltpu.CompilerParams(vmem_limit_bytes=VMEM_LIMIT_BYTES),
        name="mixer_sample",
    )(z, cbuf, h0, cw, cb, wg, bg, lam, ln, swx, sbx)


def _mem_kv_kernel(x_ref, g_ref, w_ref, kt_ref, vb_ref, ktile_ref, vtile_ref, *, nh):
    mem = x_ref.shape[0]
    d = w_ref.shape[1] // 2
    hd = d // nh
    chunks = hd // LANES
    kv = jnp.dot(_rms(x_ref[...], g_ref[...]).astype(BF16), w_ref[...], preferred_element_type=F32)
    k, v = kv[:, :d], kv[:, d:]
    kt_ref[...] = k.T.astype(BF16)
    vb_ref[...] = v.astype(BF16)
    for tile_ref, val in ((ktile_ref, k), (vtile_ref, v)):
        for h in range(nh):
            for c in range(chunks):
                lanes = slice(h * hd + c * LANES, h * hd + (c + 1) * LANES)
                tile_ref[pl.ds(c * nh + h, mem, stride=nh * chunks), :] = val[:, lanes]


def _mem_kv(memf, g, w_kv, nbatch):
    depth, d, d2 = w_kv.shape
    mem = memf.shape[0] // nbatch
    rows = mem * d // LANES
    per_lb = lambda *blk: pl.BlockSpec((None, None) + blk, lambda l, b: (l, b, 0, 0))
    return pl.pallas_call(
        functools.partial(_mem_kv_kernel, nh=XA_HEADS),
        grid=(depth, nbatch),
        in_specs=[
            pl.BlockSpec((mem, d), lambda l, b: (b, 0)),
            pl.BlockSpec((None, 1, d), lambda l, b: (l, 0, 0)),
            pl.BlockSpec((None, d, d2), lambda l, b: (l, 0, 0)),
        ],
        out_specs=[per_lb(d, mem), per_lb(mem, d), per_lb(rows, LANES), per_lb(rows, LANES)],
        out_shape=[
            jax.ShapeDtypeStruct((depth, nbatch, d, mem), BF16),
            jax.ShapeDtypeStruct((depth, nbatch, mem, d), BF16),
            jax.ShapeDtypeStruct((depth, nbatch, rows, LANES), F32),
            jax.ShapeDtypeStruct((depth, nbatch, rows, LANES), F32),
        ],
        compiler_params=_params("arbitrary", "arbitrary"),
        name="mem_kv",
    )(memf, g, w_kv)


def _tiles_to_heads(t, mem, nh):
    depth, nbatch, rows, _ = t.shape
    chunks = rows // (mem * nh)
    t = t.reshape(depth, nbatch, mem, chunks, nh, LANES)
    return jnp.transpose(t, (0, 1, 2, 4, 3, 5)).reshape(depth, nbatch, mem, nh, chunks * LANES)


def _softmax_rows(s):
    e = jnp.exp(s - jnp.max(s, axis=-1, keepdims=True))
    return e / jnp.sum(e, axis=-1, keepdims=True)


def _attn_prompt_kernel(q_ref, kt_ref, v_ref, os_ref, o_ref, *, n_prompt, hd, scale):
    i = pl.program_id(0)

    @pl.when(i < n_prompt)
    def _():
        heads = [slice(h * hd, (h + 1) * hd) for h in range(XA_HEADS)]
        scores = [jnp.dot(q_ref[:, hs], kt_ref[0, hs, :], preferred_element_type=F32) * scale for hs in heads]
        for hs, s in zip(heads, scores):
            p = _softmax_rows(s).astype(BF16)
            o_ref[:, hs] = jnp.dot(p, v_ref[0, :, hs], preferred_element_type=F32).astype(BF16)

    @pl.when(i >= n_prompt)
    def _():
        o_ref[...] = os_ref[...]


def _attn_prompt(q, kt, v, layer, o_sample, nbatch, seq, tq):
    d = q.shape[1]
    mem = v.shape[2]
    hd = d // XA_HEADS
    nt = seq // tq
    n_prompt = nbatch * nt
    n_sample = o_sample.shape[0] // tq
    batch_of = lambda i: jnp.minimum(i // nt, nbatch - 1)
    return pl.pallas_call(
        functools.partial(_attn_prompt_kernel, n_prompt=n_prompt, hd=hd, scale=hd ** -0.5),
        grid=(n_prompt + n_sample,),
        in_specs=[
            pl.BlockSpec((tq, d), lambda i: (jnp.minimum(i, n_prompt - 1), 0)),
            pl.BlockSpec((None, 1, d, mem), lambda i: (layer, batch_of(i), 0, 0)),
            pl.BlockSpec((None, 1, mem, d), lambda i: (layer, batch_of(i), 0, 0)),
            pl.BlockSpec((tq, d), lambda i: (jnp.maximum(i - n_prompt, 0), 0)),
        ],
        out_specs=pl.BlockSpec((tq, d), lambda i: (i, 0)),
        out_shape=jax.ShapeDtypeStruct((nbatch * seq + o_sample.shape[0], d), BF16),
        compiler_params=_params("arbitrary"),
        name="attn_prompt",
    )(q, kt, v, o_sample)


def _attn_sample_kernel(qt_ref, k0_ref, k1_ref, v0_ref, v1_ref, o_ref, *, ts, scale, steps_per_group):
    gb, mem = k0_ref.shape[:2]
    nh, hd, ncol = qt_ref.shape[1:]
    sub = pl.program_id(0) % steps_per_group

    @pl.when(sub == 0)
    def _():
        o_ref[...] = jnp.zeros_like(o_ref)

    def head_rows(halves, h):
        flat = [r.reshape(gb, mem * SUBLANES, LANES) for r in halves]
        chunks = hd // LANES
        per_half = chunks // len(halves)
        return jnp.concatenate(
            [jnp.concatenate([flat[c // per_half][j, pl.ds((c % per_half) * nh + h, mem, stride=SUBLANES), :]
                              for c in range(chunks)], axis=1)
             for j in range(gb)], axis=0).astype(BF16)

    col_owner = lax.broadcasted_iota(jnp.int32, (gb, mem, ncol), 2) // ts
    own = col_owner == sub * gb + lax.broadcasted_iota(jnp.int32, (gb, mem, ncol), 0)
    def scores(h):
        st = jnp.dot(head_rows((k0_ref, k1_ref), h), qt_ref[0, h], preferred_element_type=F32) * scale
        return st.reshape(gb, mem, ncol)

    for h, st in enumerate([scores(h) for h in range(nh)]):
        e = jnp.exp(st - jnp.max(st, axis=1, keepdims=True))
        p = e / jnp.sum(e, axis=1, keepdims=True)
        pm = jnp.where(own, p, 0.0).astype(BF16).reshape(gb * mem, ncol)
        o = lax.dot_general(pm, head_rows((v0_ref, v1_ref), h), (((0,), (0,)), ((), ())),
                            preferred_element_type=F32)
        o_ref[0, :, h * hd:(h + 1) * hd] += o


def _attn_sample(qt, k_tiles, v_tiles, layer, ts):
    ngroups, nh, hd, ncol = qt.shape
    _, nb, mem, halves, _, _ = k_tiles.shape
    assert halves == 2 and nh * (hd // LANES) == halves * SUBLANES
    gb = CACHE_STEP_BATCH
    steps_per_group = nb // ngroups // gb

    def half_spec(half):
        return pl.BlockSpec((None, gb, mem, None, SUBLANES, LANES), lambda i: (layer, i, 0, half, 0, 0))

    return pl.pallas_call(
        functools.partial(_attn_sample_kernel, ts=ts, scale=hd ** -0.5, steps_per_group=steps_per_group),
        grid=(nb // gb,),
        in_specs=[
            pl.BlockSpec((1, nh, hd, ncol), lambda i: (i // steps_per_group, 0, 0, 0)),
            half_spec(0), half_spec(1), half_spec(0), half_spec(1),
        ],
        out_specs=pl.BlockSpec((1, ncol, nh * hd), lambda i: (i // steps_per_group, 0, 0)),
        out_shape=jax.ShapeDtypeStruct((ngroups, ncol, nh * hd), F32),
        compiler_params=_params("arbitrary"),
        name="attn_sample",
    )(qt, k_tiles, k_tiles, v_tiles, v_tiles)


def _cache_tiles(cache):
    depth, nb, mem, nh, hd = cache.shape
    c = cache.reshape(depth, nb, mem, nh, hd // LANES, LANES)
    return jnp.transpose(c, (0, 1, 2, 4, 3, 5)).reshape(depth, nb, mem, -1, SUBLANES, LANES)


def _pad_to(a, axis, size):
    pad = [(0, 0)] * a.ndim
    pad[axis] = (0, size - a.shape[axis])
    return jnp.pad(a, pad)


def _gate_weights(w_a, b_a, w_i, b_i):
    heads, hd, _ = w_a.shape
    per = LRU_PAIR // hd
    npair = heads // per

    def blockdiag(w):
        w = w.reshape(npair, per, hd, hd)
        eye = jnp.eye(per, dtype=w.dtype)
        return jnp.einsum('pade,ab->padbe', w, eye).reshape(npair, per * hd, per * hd)

    wg = jnp.concatenate([blockdiag(w_a), blockdiag(w_i)], axis=-1).astype(BF16)
    bg = jnp.concatenate([b_a.reshape(npair, 1, per * hd), b_i.reshape(npair, 1, per * hd)], axis=-1)
    return wg, bg


def kernel(x_prompt, x_sample, mem_prompt, cache_mem_k, cache_mem_v, state_conv, state_lru_h, ffn1_norm, ffn1_w_in, ffn1_w_down, mix_norm, w_in, conv_w, conv_b, lru_w_a, lru_b_a, lru_w_i, lru_b_i, lru_lambda, sgu_ln, sgu_w, sgu_b, w_out, xa_norm, xa_w_q, xa_w_kv, xa_w_o, ffn2_norm, ffn2_w_in, ffn2_w_down):
    nbatch, seq, d = x_prompt.shape
    nb, ts, _ = x_sample.shape
    depth = ffn1_norm.shape[0]
    mem = mem_prompt.shape[1]
    dl = conv_w.shape[-1]
    ds = sgu_ln.shape[-1]
    hd = d // XA_HEADS
    d_ff = ffn1_w_down.shape[1]
    mp, ms = nbatch * seq, nb * ts
    assert seq % CHUNK == 0 and ts < CHUNK and ts >= CONV_WIDTH - 1
    assert nb % SAMPLE_GROUP == 0 and SAMPLE_GROUP * ts <= LANES
    assert dl % LRU_PAIR == 0 and ds % SGU_HEADS == 0

    tm = _pick_tile(math.gcd(mp, ms), (512, 256, 128, 64))
    tt = _pick_tile(math.gcd(seq, ms), (256, 128))
    tq = _pick_tile(math.gcd(seq, ms), (512, 256, 128, 64))
    ngroups = nb // SAMPLE_GROUP

    x_p, x_s = x_prompt.reshape(mp, d), jnp.transpose(x_sample, (1, 0, 2)).reshape(ms, d)

    ffn_w = [ffn1_w_in[0:1].astype(BF16), ffn1_w_down[0:1].astype(BF16), None, None]
    w_in_b, w_kv_b = w_in.astype(BF16), xa_w_kv.astype(BF16)
    k_tiles, v_tiles = _cache_tiles(cache_mem_k), _cache_tiles(cache_mem_v)

    kt_p, v_p, ktile_p, vtile_p = _mem_kv(mem_prompt.reshape(nbatch * mem, d), xa_norm[:, 2:3], w_kv_b, nbatch)

    outs = {k: [] for k in ("cvp", "hp", "cvs", "hs", "vs")}
    for l in range(depth):
        ffn1_wi, ffn1_wd = ffn_w[:2]
        if l == 0:
            x = _ffn(x_p, ffn1_norm[l:l + 1], ffn1_wi, ffn1_wd, 0, tm, xs=x_s)
        else:
            x = _ffn(x, ffn1_norm[l:l + 1], ffn1_wi, ffn1_wd, 0, tm)

        wg, bg = _gate_weights(lru_w_a[l], lru_b_a[l], lru_w_i[l], lru_b_i[l])
        cw, cb, lam = conv_w[l], conv_b[l][None, :], lru_lambda[l][None, :]
        sb_rows = jnp.repeat(sgu_b[l].T, ds // SGU_HEADS, axis=1)
        tri = jnp.tril(jnp.ones((ts, ts), F32))
        swx = jnp.repeat(jnp.transpose(sgu_w[l][:, :ts, :ts] * tri, (1, 2, 0)).reshape(ts * ts, SGU_HEADS),
                         ds // SGU_HEADS, axis=1)
        z_s = _norm_matmul(x, mix_norm[l, 0:1], w_in_b, l, tm, _pick_tile(w_in.shape[2], (2048, 1024)), F32,
                           "mix_in_sample", row_block0=mp // tm, rows=ms)
        y_s, conv_s, h_s, v_s = _mixer_sample(
            z_s, jnp.transpose(state_conv[l], (1, 0, 2)), state_lru_h[l], cw, cb, wg, bg, lam, sgu_ln[l],
            swx, sb_rows[:ts], nb, ts)
        y, xlast, hlast, *cast_w = _mix_prompt(
            x, mix_norm[l, 0:1], w_in_b, l, y_s, nbatch, seq, cw, cb, wg, bg, lam, sgu_ln[l], sgu_w[l], sb_rows, tt,
            cast=[(ffn2_w_in, 0), (ffn2_w_down, 0)] if l == 0 else ())
        if l == 0:
            ffn_w[2:] = [a[None] for a in cast_w]
        x = _matmul_norm_res(y, w_out, l, x, mix_norm[l, 1:2], tm, "mix_out")
        outs["cvp"].append(xlast[:, SUBLANES - (CONV_WIDTH - 1):, :])
        outs["hp"].append(hlast[:, 0, :])
        outs["cvs"].append(jnp.transpose(conv_s, (1, 0, 2)))
        outs["hs"].append(h_s)
        outs["vs"].append(jnp.transpose(v_s, (1, 0, 2)))

        q = _norm_matmul_f32w(x, xa_norm[l, 0:1], xa_w_q, l, tm, BF16, "xa_q")
        qt = q[mp:].reshape(ts, ngroups, SAMPLE_GROUP, XA_HEADS, hd)
        qt = jnp.transpose(qt, (1, 3, 4, 2, 0)).reshape(ngroups, XA_HEADS, hd, SAMPLE_GROUP * ts)
        o_s = _attn_sample(qt, k_tiles, v_tiles, l, ts).astype(BF16)
        o_s = jnp.transpose(o_s.reshape(ngroups, SAMPLE_GROUP, ts, d), (2, 0, 1, 3)).reshape(ms, d)
        o = _attn_prompt(q, kt_p, v_p, l, o_s, nbatch, seq, tq)
        x = _matmul_norm_res(o, xa_w_o, l, x, xa_norm[l, 1:2], tm, "xa_out")

        ffn2_wi, ffn2_wd = ffn_w[2:]
        if l < depth - 1:
            nxt = [(a, l + 1) for a in (ffn1_w_in, ffn1_w_down, ffn2_w_in, ffn2_w_down)]
            x, *ffn_w = _ffn(x, ffn2_norm[l:l + 1], ffn2_wi, ffn2_wd, 0, tm, cast=nxt)
            ffn_w = [a[None] for a in ffn_w]

    last_norm = ffn2_norm[depth - 1:depth]
    y_prompt = _ffn(x, last_norm, ffn2_wi, ffn2_wd, 0, tm, rows=mp).reshape(nbatch, seq, d)
    y_sample = _ffn(x, last_norm, ffn2_wi, ffn2_wd, 0, tm, row_block0=mp // tm, rows=ms)
    y_sample = jnp.transpose(y_sample.reshape(ts, nb, d), (1, 0, 2))
    return (y_prompt, y_sample, _tiles_to_heads(ktile_p, mem, XA_HEADS), _tiles_to_heads(vtile_p, mem, XA_HEADS),
            jnp.stack(outs["cvp"]),
            jnp.stack(outs["hp"]), jnp.stack(outs["cvs"]), jnp.stack(outs["hs"]), jnp.stack(outs["vs"]))
```

```python
import functools
import math

import jax
import jax.numpy as jnp
from jax import lax
from jax.experimental import pallas as pl
from jax.experimental.pallas import tpu as pltpu

F32 = jnp.float32
BF16 = jnp.bfloat16

EPS = 1e-6
HALF = 0.5
LRU_C = 8.0
CONV_WIDTH = 4
CHUNK = 128
SGU_HEADS = 8
XA_HEADS = 4

VMEM_LIMIT_BYTES = 56 * 1024 * 1024
SUBLANES = 8
LANES = 128
BF16_ROWS = 16
FF_TILE = 1024
SAMPLE_GROUP = 8
CACHE_STEP_BATCH = 4
CACHE_STREAM_PARTS = 2
LRU_PAIR = 2 * LANES


def _params(*sem):
    return pltpu.CompilerParams(dimension_semantics=sem, vmem_limit_bytes=VMEM_LIMIT_BYTES)


def _pick_tile(n, prefs):
    for p in prefs:
        if n % p == 0:
            return p
    raise ValueError(f"no tile in {prefs} divides {n}")


def _rms(x, g):
    return (x * lax.rsqrt(jnp.mean(x * x, axis=-1, keepdims=True) + EPS)) * g


def _gelu(x):
    c = math.sqrt(2.0 / math.pi)
    return 0.5 * x * (1.0 + jnp.tanh(c * (x + 0.044715 * (x * x * x))))


def _softplus(x):
    return jnp.maximum(x, 0.0) + jnp.log1p(jnp.exp(-jnp.abs(x)))


def _ffn_kernel(x_ref, *refs, nf, last_cols, n_first, two_sources, ncast):
    i, f = pl.program_id(0), pl.program_id(1)
    refs = list(refs)
    xs_ref = refs.pop(0) if two_sources else None
    g_ref, wg_ref, wu_ref, wd_ref = refs[:4]
    cast_in, o_ref, cast_out, h_scr = refs[4:4 + ncast], refs[4 + ncast], refs[5 + ncast:5 + 2 * ncast], refs[-1]
    if two_sources:
        read_x = lambda: jnp.where(i < n_first, x_ref[...], xs_ref[...])
    else:
        read_x = lambda: x_ref[...]

    @pl.when(f == 0)
    def _():
        h_scr[...] = _rms(read_x(), g_ref[0:1, :]).astype(BF16)
        o_ref[...] = jnp.zeros_like(o_ref)

    def accumulate(cols):
        h = h_scr[...]
        gate = jnp.dot(h, wg_ref[:, :cols], preferred_element_type=F32)
        up = jnp.dot(h, wu_ref[0, :, FF_TILE - cols:], preferred_element_type=F32)
        act = ((gate * jax.nn.sigmoid(gate)) * up).astype(BF16)
        o_ref[...] += jnp.dot(act, wd_ref[:cols, :], preferred_element_type=F32)
        for src, dst in zip(cast_in, cast_out):
            dst[...] = src[...].astype(BF16)

    if last_cols == FF_TILE:
        accumulate(FF_TILE)
    else:
        @pl.when(f < nf - 1)
        def _():
            accumulate(FF_TILE)

        @pl.when(f == nf - 1)
        def _():
            accumulate(last_cols)

    @pl.when(f == nf - 1)
    def _():
        o_ref[...] = read_x() + HALF * _rms(o_ref[...], g_ref[1:2, :])


def _cast_blocking(shape, nsteps):
    r, c = shape
    ncb = 2 if c > 4096 and (c // 2) % LANES == 0 and nsteps >= 2 else 1
    br = -(-(-(-r // (nsteps // ncb))) // BF16_ROWS) * BF16_ROWS
    assert -(-r // br) * ncb <= nsteps
    return br, c // ncb, ncb


def _cast_specs(cast, nsteps, step_of):
    in_specs, out_specs = [], []
    for a, src_layer in cast:
        br, bc, ncb = _cast_blocking(a.shape[1:], nsteps)
        last = -(-a.shape[1] // br) * ncb - 1

        def blk(*idx, ncb=ncb, last=last):
            step = jnp.minimum(step_of(*idx), last)
            return step // ncb, step % ncb

        in_specs.append(pl.BlockSpec((None, br, bc), lambda *idx, blk=blk, sl=src_layer: (sl,) + blk(*idx)))
        out_specs.append(pl.BlockSpec((br, bc), blk))
    return in_specs, out_specs


def _ffn(x, norm, w_in, wd, layer, tm, xs=None, row_block0=0, rows=None, cast=()):
    d = x.shape[1]
    n_first = (x.shape[0] if rows is None else rows) // tm
    n_second = 0 if xs is None else xs.shape[0] // tm
    second = [] if xs is None else [(xs, pl.BlockSpec((tm, d), lambda i, f: (jnp.maximum(i - n_first, 0), 0)))]
    ff = wd.shape[1]
    assert ff % LANES == 0 and FF_TILE % LANES == 0 and ff >= FF_TILE
    nf = pl.cdiv(ff, FF_TILE)
    up_start = lambda f: jnp.minimum(ff // LANES + f * (FF_TILE // LANES), (2 * ff - FF_TILE) // LANES) * LANES
    cast_in_specs, cast_out_specs = _cast_specs(cast, (n_first + n_second) * nf, lambda i, f: i * nf + f)
    res = pl.pallas_call(
        functools.partial(_ffn_kernel, nf=nf, last_cols=ff - (nf - 1) * FF_TILE, n_first=n_first,
                          two_sources=xs is not None, ncast=len(cast)),
        grid=(n_first + n_second, nf),
        in_specs=[
            pl.BlockSpec((tm, d), lambda i, f: (jnp.minimum(i, n_first - 1) + row_block0, 0)),
            *[spec for _, spec in second],
            pl.BlockSpec((None, 2, d), lambda i, f: (layer, 0, 0)),
            pl.BlockSpec((None, d, FF_TILE), lambda i, f: (layer, 0, f)),
            pl.BlockSpec((pl.Element(1), pl.Element(d), pl.Element(FF_TILE)), lambda i, f: (layer, 0, up_start(f))),
            pl.BlockSpec((None, FF_TILE, d), lambda i, f: (layer, f, 0)),
            *cast_in_specs,
        ],
        out_specs=[pl.BlockSpec((tm, d), lambda i, f: (i, 0)), *cast_out_specs],
        out_shape=[jax.ShapeDtypeStruct(((n_first + n_second) * tm, d), F32),
                   *[jax.ShapeDtypeStruct(a.shape[1:], BF16) for a, _ in cast]],
        scratch_shapes=[pltpu.VMEM((tm, d), BF16)],
        compiler_params=_params("arbitrary", "arbitrary"),
        name="ffn",
    )(x, *[a for a, _ in second], norm, w_in, w_in, wd, *[a for a, _ in cast])
    return res if cast else res[0]


def _round_weight_once(w_ref, w_scr, first_step):
    @pl.when(first_step)
    def _():
        w_scr[...] = w_ref[...].astype(BF16)


def _norm_matmul_kernel(x_ref, g_ref, w_ref, o_ref, h_scr):
    @pl.when(pl.program_id(1) == 0)
    def _():
        h_scr[...] = _rms(x_ref[...], g_ref[...]).astype(BF16)

    o_ref[...] = jnp.dot(h_scr[...], w_ref[...], preferred_element_type=F32).astype(o_ref.dtype)


def _norm_matmul(x, g, w, layer, tm, tn, out_dtype, name, row_block0=0, rows=None):
    d = x.shape[1]
    m = x.shape[0] if rows is None else rows
    n = w.shape[2]
    return pl.pallas_call(
        _norm_matmul_kernel,
        grid=(m // tm, n // tn),
        in_specs=[
            pl.BlockSpec((tm, d), lambda i, j: (i + row_block0, 0)),
            pl.BlockSpec((1, d), lambda i, j: (0, 0)),
            pl.BlockSpec((None, d, tn), lambda i, j: (layer, 0, j)),
        ],
        out_specs=pl.BlockSpec((tm, tn), lambda i, j: (i, j)),
        out_shape=jax.ShapeDtypeStruct((m, n), out_dtype),
        scratch_shapes=[pltpu.VMEM((tm, d), BF16)],
        compiler_params=_params("parallel", "arbitrary"),
        name=name,
    )(x, g, w)


def _norm_matmul_f32w_kernel(x_ref, g_ref, w_ref, o_ref, w_scr):
    _round_weight_once(w_ref, w_scr, pl.program_id(0) == 0)
    h = _rms(x_ref[...], g_ref[...]).astype(BF16)
    o_ref[...] = jnp.dot(h, w_scr[...], preferred_element_type=F32).astype(o_ref.dtype)


def _norm_matmul_f32w(x, g, w, layer, tm, out_dtype, name):
    m, d = x.shape
    n = w.shape[2]
    return pl.pallas_call(
        _norm_matmul_f32w_kernel,
        grid=(m // tm,),
        in_specs=[
            pl.BlockSpec((tm, d), lambda i: (i, 0)),
            pl.BlockSpec((1, d), lambda i: (0, 0)),
            pl.BlockSpec((None, d, n), lambda i: (layer, 0, 0), pipeline_mode=pl.Buffered(1)),
        ],
        out_specs=pl.BlockSpec((tm, n), lambda i: (i, 0)),
        out_shape=jax.ShapeDtypeStruct((m, n), out_dtype),
        scratch_shapes=[pltpu.VMEM((d, n), BF16)],
        compiler_params=_params("arbitrary"),
        name=name,
    )(x, g, w)


def _matmul_norm_res_kernel(y_ref, w_ref, x_ref, g_ref, o_ref, w_scr):
    _round_weight_once(w_ref, w_scr, pl.program_id(0) == 0)
    d = jnp.dot(y_ref[...], w_scr[...], preferred_element_type=F32)
    o_ref[...] = x_ref[...] + _rms(d, g_ref[...])


def _matmul_norm_res(y, w, layer, x, g, tm, name):
    m, k = y.shape
    n = w.shape[2]
    return pl.pallas_call(
        _matmul_norm_res_kernel,
        grid=(m // tm,),
        in_specs=[
            pl.BlockSpec((tm, k), lambda i: (i, 0)),
            pl.BlockSpec((None, k, n), lambda i: (layer, 0, 0), pipeline_mode=pl.Buffered(1)),
            pl.BlockSpec((tm, n), lambda i: (i, 0)),
            pl.BlockSpec((1, n), lambda i: (0, 0)),
        ],
        out_specs=pl.BlockSpec((tm, n), lambda i: (i, 0)),
        out_shape=jax.ShapeDtypeStruct((m, n), F32),
        scratch_shapes=[pltpu.VMEM((k, n), BF16)],
        compiler_params=_params("arbitrary"),
        name=name,
    )(y, w, x, g)


def _lru_gates(xc, wg, bg, c):
    g = jnp.dot(xc.astype(BF16), wg, preferred_element_type=F32) + bg
    r = jax.nn.sigmoid(g[:, :LRU_PAIR])
    i = jax.nn.sigmoid(g[:, LRU_PAIR:])
    log_a = c * r
    a = jnp.exp(log_a)
    b = jnp.sqrt(-jnp.tanh(log_a) * (a * a + 1.0)) * (i * xc)
    return a, b


def _layer_norm(x, g, b):
    mu = jnp.mean(x, axis=-1, keepdims=True)
    xc = x - mu
    return (xc * lax.rsqrt(jnp.mean(xc * xc, axis=-1, keepdims=True) + EPS)) * g + b


def _mix_prompt_kernel(x_ref, g_ref, w_ref, cw_ref, cb_ref, wg_ref, bg_ref, lam_ref, ln_ref, sw_ref, sb_ref, ys_ref,
                       *refs, nt, n_prompt, tt, dl, ds, ncast):
    cast_in, (y_ref, xlast_ref, hlast_ref) = refs[:ncast], refs[ncast:ncast + 3]
    cast_out, (z_a, z_b, xprev_scr, hc_scr, xc_scr) = refs[ncast + 3:2 * ncast + 3], refs[2 * ncast + 3:]
    i = pl.program_id(0)
    t = (i + nt - 1) % nt

    @pl.when(t == 0)
    def _():
        xprev_scr[...] = jnp.zeros_like(xprev_scr)
        hc_scr[...] = jnp.zeros_like(hc_scr)

    def round_side_blocks():
        for src, dst in zip(cast_in, cast_out):
            dst[...] = src[...].astype(BF16)

    def step(z_write, z_read):
        dz = w_ref.shape[1]
        if z_write is None:
            project = lambda k, of: None
        else:
            h = _rms(x_ref[...], g_ref[...]).astype(BF16)

            def project(k, of):
                cols = slice(k * dz // of, (k + 1) * dz // of)
                z_write[:, cols] = jnp.dot(h, w_ref[:, cols], preferred_element_type=F32)

        if z_read is None:
            project(0, 1)
        else:
            _mixer_prompt_tile(z_read, cw_ref, cb_ref, wg_ref, bg_ref, lam_ref, ln_ref, sw_ref, sb_ref,
                               y_ref, xlast_ref, hlast_ref, xprev_scr, hc_scr, xc_scr, project, tt=tt, dl=dl, ds=ds)
        round_side_blocks()

    z = (z_a, z_b)
    middle = (i > 0) & (i < n_prompt)

    @pl.when(i == 0)
    def _():
        step(z[0], None)

    @pl.when(middle & (i % 2 == 0))
    def _():
        step(z[0], z[1])

    @pl.when(middle & (i % 2 == 1))
    def _():
        step(z[1], z[0])

    @pl.when(i == n_prompt)
    def _():
        step(None, z[(n_prompt - 1) % 2])

    @pl.when(i > n_prompt)
    def _():
        y_ref[...] = ys_ref[...]


def _mixer_prompt_tile(z_ref, cw_ref, cb_ref, wg_ref, bg_ref, lam_ref, ln_ref, sw_ref, sb_ref,
                       y_ref, xlast_ref, hlast_ref, xprev_scr, hc_scr, xc_scr, project, *, tt, dl, ds):
    ng = tt // SUBLANES
    npairs = dl // LRU_PAIR
    nslices = 4 * npairs
    project(0, nslices)
    project(1, nslices)

    x = z_ref[:, 0:dl]
    w = cw_ref[...]
    cb = cb_ref[...]
    acc = cb + w[CONV_WIDTH - 1:CONV_WIDTH, :] * x
    for k in range(1, CONV_WIDTH):
        acc = acc + w[CONV_WIDTH - 1 - k:CONV_WIDTH - k, :] * pltpu.roll(x, k, 0)
    xc_scr[...] = acc
    x_head = x[0:SUBLANES, :]
    ext = jnp.concatenate([xprev_scr[...], x_head], axis=0)
    acc = cb + w[CONV_WIDTH - 1:CONV_WIDTH, :] * x_head
    for k in range(1, CONV_WIDTH):
        acc = acc + w[CONV_WIDTH - 1 - k:CONV_WIDTH - k, :] * pltpu.roll(ext, k, 0)[SUBLANES:, :]
    xc_scr[0:SUBLANES, :] = acc
    x_tail = x[tt - SUBLANES:tt, :]
    xprev_scr[...] = x_tail
    xlast_ref[0] = x_tail

    c_all = -LRU_C * _softplus(-lam_ref[...])
    sidx = lax.broadcasted_iota(jnp.int32, (ng, SUBLANES, LRU_PAIR), 1)
    for p in range(npairs):
        sl = slice(p * LRU_PAIR, (p + 1) * LRU_PAIR)
        project(2 + 2 * p, nslices)
        a, b = _lru_gates(xc_scr[:, sl], wg_ref[p], bg_ref[p], c_all[:, sl])
        a = a.reshape(ng, SUBLANES, LRU_PAIR)
        b = b.reshape(ng, SUBLANES, LRU_PAIR)
        shift = 1
        while shift < SUBLANES:
            keep = sidx >= shift
            b = jnp.where(keep, a * pltpu.roll(b, shift, 1) + b, b)
            a = jnp.where(keep, a * pltpu.roll(a, shift, 1), a)
            shift *= 2
        project(3 + 2 * p, nslices)
        carry = hc_scr[:, sl]
        groups = []
        for gi in range(ng):
            hg = a[gi] * carry + b[gi]
            groups.append(hg)
            carry = hg[SUBLANES - 1:SUBLANES, :]
        hc_scr[:, sl] = carry
        hlast_ref[0, :, sl] = carry
        h_seq = jnp.concatenate(groups, axis=0)
        y_ref[:, sl] = (h_seq * _gelu(z_ref[:, dl + p * LRU_PAIR:dl + (p + 1) * LRU_PAIR])).astype(BF16)

    project(2 * npairs + 2, nslices)
    project(2 * npairs + 3, nslices)
    v = _layer_norm(_gelu(z_ref[:, 2 * dl + ds:2 * dl + 2 * ds]), ln_ref[0:1, :], ln_ref[1:2, :])
    vb = v.astype(BF16)
    project(2 * npairs + 4, nslices)
    project(2 * npairs + 5, nslices)
    u = _gelu(z_ref[:, 2 * dl:2 * dl + ds])
    nc = tt // CHUNK
    hd = ds // SGU_HEADS
    row = lax.broadcasted_iota(jnp.int32, (CHUNK, CHUNK), 0)
    col = lax.broadcasted_iota(jnp.int32, (CHUNK, CHUNK), 1)
    for g in range(SGU_HEADS):
        if g == SGU_HEADS // 2:
            project(2 * npairs + 6, nslices)
            project(2 * npairs + 7, nslices)
        cs = slice(g * hd, (g + 1) * hd)
        wt = jnp.where(row >= col, sw_ref[g], 0.0).astype(BF16)
        vg = jnp.concatenate([vb[c * CHUNK:(c + 1) * CHUNK, cs] for c in range(nc)], axis=1)
        s = jnp.dot(wt, vg, preferred_element_type=F32)
        for c in range(nc):
            rs = slice(c * CHUNK, (c + 1) * CHUNK)
            sc = s[:, c * hd:(c + 1) * hd] + sb_ref[:, cs]
            y_ref[rs, dl + g * hd:dl + (g + 1) * hd] = (u[rs, cs] * sc).astype(BF16)
    assert 2 * npairs + 8 == nslices


def _mix_prompt(x, g, w_in, layer, ys, nbatch, seq, cw, cb, wg, bg, lam, ln, sw, sb, tt, cast=()):
    d = x.shape[1]
    dz = w_in.shape[2]
    dl = cw.shape[1]
    ds = ln.shape[1]
    nt = seq // tt
    n_prompt = nbatch * nt
    n_sample = ys.shape[0] // tt
    full = lambda *shape: pl.BlockSpec(shape, lambda i: (0,) * len(shape))
    batch_of = lambda i: jnp.clip((i - 1) // nt, 0, nbatch - 1)
    cast_in_specs, cast_out_specs = _cast_specs(cast, n_prompt + 1, lambda i: i)
    return pl.pallas_call(
        functools.partial(_mix_prompt_kernel, nt=nt, n_prompt=n_prompt, tt=tt, dl=dl, ds=ds, ncast=len(cast)),
        grid=(n_prompt + 1 + n_sample,),
        in_specs=[
            pl.BlockSpec((tt, d), lambda i: (jnp.minimum(i, n_prompt - 1), 0)),
            full(*g.shape),
            pl.BlockSpec((None, d, dz), lambda i: (layer, 0, 0), pipeline_mode=pl.Buffered(1)),
            full(*cw.shape), full(*cb.shape), full(*wg.shape), full(*bg.shape), full(*lam.shape),
            full(*ln.shape), full(*sw.shape), full(*sb.shape),
            pl.BlockSpec((tt, dl + ds), lambda i: (jnp.clip(i - n_prompt - 1, 0, n_sample - 1), 0)),
            *cast_in_specs,
        ],
        out_specs=[
            pl.BlockSpec((tt, dl + ds), lambda i: (jnp.maximum(i - 1, 0), 0)),
            pl.BlockSpec((1, SUBLANES, dl), lambda i: (batch_of(i), 0, 0)),
            pl.BlockSpec((1, 1, dl), lambda i: (batch_of(i), 0, 0)),
            *cast_out_specs,
        ],
        out_shape=[
            jax.ShapeDtypeStruct((nbatch * seq + ys.shape[0], dl + ds), BF16),
            jax.ShapeDtypeStruct((nbatch, SUBLANES, dl), F32),
            jax.ShapeDtypeStruct((nbatch, 1, dl), F32),
            *[jax.ShapeDtypeStruct(a.shape[1:], BF16) for a, _ in cast],
        ],
        scratch_shapes=[
            pltpu.VMEM((tt, dz), F32),
            pltpu.VMEM((tt, dz), F32),
            pltpu.VMEM((SUBLANES, dl), F32),
            pltpu.VMEM((1, dl), F32),
            pltpu.VMEM((tt, dl), F32),
        ],
        compiler_params=_params("arbitrary"),
        name="mix_prompt",
    )(x, g, w_in, cw, cb, wg, bg, lam, ln, sw, sb, ys, *[a for a, _ in cast])


def _mixer_sample_kernel(z_ref, cbuf_ref, h0_ref, cw_ref, cb_ref, wg_ref, bg_ref, lam_ref, ln_ref,
                         swx_ref, sbx_ref,
                         y_ref, convnew_ref, hlast_ref, v_ref, xc_scr, *, nb, ts, dl, ds):
    rows = lambda t: slice(t * nb, (t + 1) * nb)
    hist = CONV_WIDTH - 1
    xx = [cbuf_ref[k] for k in range(hist)] + [z_ref[rows(t), 0:dl] for t in range(ts)]
    w = cw_ref[...]
    cb = cb_ref[...]
    for t in range(ts):
        acc = cb + w[0:1, :] * xx[t]
        for k in range(1, CONV_WIDTH):
            acc = acc + w[k:k + 1, :] * xx[t + k]
        xc_scr[rows(t), :] = acc
    for k in range(hist):
        convnew_ref[k] = xx[ts + k]

    c_all = -LRU_C * _softplus(-lam_ref[...])
    for p in range(dl // LRU_PAIR):
        sl = slice(p * LRU_PAIR, (p + 1) * LRU_PAIR)
        a, b = _lru_gates(xc_scr[:, sl], wg_ref[p], bg_ref[p], c_all[:, sl])
        h = h0_ref[:, sl]
        for t in range(ts):
            h = a[rows(t), :] * h + b[rows(t), :]
            gl = z_ref[rows(t), dl + p * LRU_PAIR:dl + (p + 1) * LRU_PAIR]
            y_ref[rows(t), sl] = (h * _gelu(gl)).astype(BF16)
        hlast_ref[:, sl] = h

    vs = []
    for t in range(ts):
        v = _layer_norm(_gelu(z_ref[rows(t), 2 * dl + ds:2 * dl + 2 * ds]), ln_ref[0:1, :], ln_ref[1:2, :])
        v_ref[t] = v
        vs.append(v)
    for t in range(ts):
        s = sbx_ref[t:t + 1, :] + swx_ref[t * ts:t * ts + 1, :] * vs[0]
        for j in range(1, t + 1):
            s = s + swx_ref[t * ts + j:t * ts + j + 1, :] * vs[j]
        u = _gelu(z_ref[rows(t), 2 * dl:2 * dl + ds])
        y_ref[rows(t), dl:dl + ds] = (u * s).astype(BF16)


def _mixer_sample(z, cbuf, h0, cw, cb, wg, bg, lam, ln, swx, sbx, nb, ts):
    dl = cw.shape[1]
    ds = ln.shape[1]
    return pl.pallas_call(
        functools.partial(_mixer_sample_kernel, nb=nb, ts=ts, dl=dl, ds=ds),
        out_shape=[
            jax.ShapeDtypeStruct((ts * nb, dl + ds), BF16),
            jax.ShapeDtypeStruct((CONV_WIDTH - 1, nb, dl), F32),
            jax.ShapeDtypeStruct((nb, dl), F32),
            jax.ShapeDtypeStruct((ts, nb, ds), F32),
        ],
        scratch_shapes=[pltpu.VMEM((ts * nb, dl), F32)],
        compiler_params=pltpu.CompilerParams(vmem_limit_bytes=VMEM_LIMIT_BYTES),
        name="mixer_sample",
    )(z, cbuf, h0, cw, cb, wg, bg, lam, ln, swx, sbx)


def _mem_kv_kernel(x_ref, g_ref, w_ref, kt_ref, vb_ref, ktile_ref, vtile_ref, *, nh):
    mem = x_ref.shape[0]
    d = w_ref.shape[1] // 2
    hd = d // nh
    chunks = hd // LANES
    kv = jnp.dot(_rms(x_ref[...], g_ref[...]).astype(BF16), w_ref[...], preferred_element_type=F32)
    k, v = kv[:, :d], kv[:, d:]
    kt_ref[...] = k.T.astype(BF16)
    vb_ref[...] = v.astype(BF16)
    for tile_ref, val in ((ktile_ref, k), (vtile_ref, v)):
        for h in range(nh):
            for c in range(chunks):
                lanes = slice(h * hd + c * LANES, h * hd + (c + 1) * LANES)
                tile_ref[pl.ds(c * nh + h, mem, stride=nh * chunks), :] = val[:, lanes]


def _mem_kv(memf, g, w_kv, nbatch):
    depth, d, d2 = w_kv.shape
    mem = memf.shape[0] // nbatch
    rows = mem * d // LANES
    per_lb = lambda *blk: pl.BlockSpec((None, None) + blk, lambda l, b: (l, b, 0, 0))
    return pl.pallas_call(
        functools.partial(_mem_kv_kernel, nh=XA_HEADS),
        grid=(depth, nbatch),
        in_specs=[
            pl.BlockSpec((mem, d), lambda l, b: (b, 0)),
            pl.BlockSpec((None, 1, d), lambda l, b: (l, 0, 0)),
            pl.BlockSpec((None, d, d2), lambda l, b: (l, 0, 0)),
        ],
        out_specs=[per_lb(d, mem), per_lb(mem, d), per_lb(rows, LANES), per_lb(rows, LANES)],
        out_shape=[
            jax.ShapeDtypeStruct((depth, nbatch, d, mem), BF16),
            jax.ShapeDtypeStruct((depth, nbatch, mem, d), BF16),
            jax.ShapeDtypeStruct((depth, nbatch, rows, LANES), F32),
            jax.ShapeDtypeStruct((depth, nbatch, rows, LANES), F32),
        ],
        compiler_params=_params("arbitrary", "arbitrary"),
        name="mem_kv",
    )(memf, g, w_kv)


def _tiles_to_heads(t, mem, nh):
    depth, nbatch, rows, _ = t.shape
    chunks = rows // (mem * nh)
    t = t.reshape(depth, nbatch, mem, chunks, nh, LANES)
    return jnp.transpose(t, (0, 1, 2, 4, 3, 5)).reshape(depth, nbatch, mem, nh, chunks * LANES)


def _softmax_rows(s):
    e = jnp.exp(s - jnp.max(s, axis=-1, keepdims=True))
    return e / jnp.sum(e, axis=-1, keepdims=True)


def _attn_prompt_kernel(q_ref, kt_ref, v_ref, os_ref, o_ref, *, n_prompt, hd, scale):
    i = pl.program_id(0)

    @pl.when(i < n_prompt)
    def _():
        heads = [slice(h * hd, (h + 1) * hd) for h in range(XA_HEADS)]
        scores = [jnp.dot(q_ref[:, hs], kt_ref[0, hs, :], preferred_element_type=F32) * scale for hs in heads]
        for hs, s in zip(heads, scores):
            p = _softmax_rows(s).astype(BF16)
            o_ref[:, hs] = jnp.dot(p, v_ref[0, :, hs], preferred_element_type=F32).astype(BF16)

    @pl.when(i >= n_prompt)
    def _():
        o_ref[...] = os_ref[...]


def _attn_prompt(q, kt, v, layer, o_sample, nbatch, seq, tq):
    d = q.shape[1]
    mem = v.shape[2]
    hd = d // XA_HEADS
    nt = seq // tq
    n_prompt = nbatch * nt
    n_sample = o_sample.shape[0] // tq
    batch_of = lambda i: jnp.minimum(i // nt, nbatch - 1)
    return pl.pallas_call(
        functools.partial(_attn_prompt_kernel, n_prompt=n_prompt, hd=hd, scale=hd ** -0.5),
        grid=(n_prompt + n_sample,),
        in_specs=[
            pl.BlockSpec((tq, d), lambda i: (jnp.minimum(i, n_prompt - 1), 0)),
            pl.BlockSpec((None, 1, d, mem), lambda i: (layer, batch_of(i), 0, 0)),
            pl.BlockSpec((None, 1, mem, d), lambda i: (layer, batch_of(i), 0, 0)),
            pl.BlockSpec((tq, d), lambda i: (jnp.maximum(i - n_prompt, 0), 0)),
        ],
        out_specs=pl.BlockSpec((tq, d), lambda i: (i, 0)),
        out_shape=jax.ShapeDtypeStruct((nbatch * seq + o_sample.shape[0], d), BF16),
        compiler_params=_params("arbitrary"),
        name="attn_prompt",
    )(q, kt, v, o_sample)


def _attn_sample_kernel(qt_ref, *refs, ts, scale, steps_per_group, nparts):
    nstreams = 2 * nparts
    k_refs, v_refs, o_ref = refs[:nstreams], refs[nstreams:2 * nstreams], refs[2 * nstreams]
    pb, mem = k_refs[0].shape[:2]
    gb = pb * nparts
    nh, hd, ncol = qt_ref.shape[1:]
    sub = pl.program_id(0) % steps_per_group

    @pl.when(sub == 0)
    def _():
        o_ref[...] = jnp.zeros_like(o_ref)

    def head_rows(streams, h):
        flat = [r.reshape(pb, mem * SUBLANES, LANES) for r in streams]
        chunks = hd // LANES
        per_half = chunks // 2
        return jnp.concatenate(
            [jnp.concatenate([flat[(c // per_half) * nparts + j // pb][
                j % pb, pl.ds((c % per_half) * nh + h, mem, stride=SUBLANES), :] for c in range(chunks)], axis=1)
             for j in range(gb)], axis=0).astype(BF16)

    col_owner = lax.broadcasted_iota(jnp.int32, (gb, mem, ncol), 2) // ts
    own = col_owner == sub * gb + lax.broadcasted_iota(jnp.int32, (gb, mem, ncol), 0)

    def scores(h):
        st = jnp.dot(head_rows(k_refs, h), qt_ref[0, h], preferred_element_type=F32) * scale
        return st.reshape(gb, mem, ncol)

    for h, st in enumerate([scores(h) for h in range(nh)]):
        e = jnp.exp(st - jnp.max(st, axis=1, keepdims=True))
        p = e / jnp.sum(e, axis=1, keepdims=True)
        pm = jnp.where(own, p, 0.0).astype(BF16).reshape(gb * mem, ncol)
        o = lax.dot_general(pm, head_rows(v_refs, h), (((0,), (0,)), ((), ())), preferred_element_type=F32)
        o_ref[0, :, h * hd:(h + 1) * hd] += o


def _attn_sample(qt, k_tiles, v_tiles, layer, ts):
    ngroups, nh, hd, ncol = qt.shape
    _, nb, mem, halves, _, _ = k_tiles.shape
    assert halves == 2 and nh * (hd // LANES) == halves * SUBLANES
    gb = CACHE_STEP_BATCH
    nparts = CACHE_STREAM_PARTS
    pb = gb // nparts
    steps_per_group = nb // ngroups // gb

    def stream(half, part):
        return pl.BlockSpec((None, pb, mem, None, SUBLANES, LANES),
                            lambda i: (layer, i * nparts + part, 0, half, 0, 0))

    streams = [stream(half, part) for half in range(halves) for part in range(nparts)]
    return pl.pallas_call(
        functools.partial(_attn_sample_kernel, ts=ts, scale=hd ** -0.5, steps_per_group=steps_per_group,
                          nparts=nparts),
        grid=(nb // gb,),
        in_specs=[pl.BlockSpec((1, nh, hd, ncol), lambda i: (i // steps_per_group, 0, 0, 0)), *streams, *streams],
        out_specs=pl.BlockSpec((1, ncol, nh * hd), lambda i: (i // steps_per_group, 0, 0)),
        out_shape=jax.ShapeDtypeStruct((ngroups, ncol, nh * hd), F32),
        compiler_params=_params("arbitrary"),
        name="attn_sample",
    )(qt, *[k_tiles] * len(streams), *[v_tiles] * len(streams))


def _cache_tiles(cache):
    depth, nb, mem, nh, hd = cache.shape
    c = cache.reshape(depth, nb, mem, nh, hd // LANES, LANES)
    return jnp.transpose(c, (0, 1, 2, 4, 3, 5)).reshape(depth, nb, mem, -1, SUBLANES, LANES)


def _gate_weights(w_a, b_a, w_i, b_i):
    heads, hd, _ = w_a.shape
    per = LRU_PAIR // hd
    npair = heads // per

    def blockdiag(w):
        w = w.reshape(npair, per, hd, hd)
        eye = jnp.eye(per, dtype=w.dtype)
        return jnp.einsum('pade,ab->padbe', w, eye).reshape(npair, per * hd, per * hd)

    wg = jnp.concatenate([blockdiag(w_a), blockdiag(w_i)], axis=-1).astype(BF16)
    bg = jnp.concatenate([b_a.reshape(npair, 1, per * hd), b_i.reshape(npair, 1, per * hd)], axis=-1)
    return wg, bg


def kernel(x_prompt, x_sample, mem_prompt, cache_mem_k, cache_mem_v, state_conv, state_lru_h, ffn1_norm, ffn1_w_in, ffn1_w_down, mix_norm, w_in, conv_w, conv_b, lru_w_a, lru_b_a, lru_w_i, lru_b_i, lru_lambda, sgu_ln, sgu_w, sgu_b, w_out, xa_norm, xa_w_q, xa_w_kv, xa_w_o, ffn2_norm, ffn2_w_in, ffn2_w_down):
    nbatch, seq, d = x_prompt.shape
    nb, ts, _ = x_sample.shape
    depth = ffn1_norm.shape[0]
    mem = mem_prompt.shape[1]
    dl = conv_w.shape[-1]
    ds = sgu_ln.shape[-1]
    hd = d // XA_HEADS
    mp, ms = nbatch * seq, nb * ts
    assert seq % CHUNK == 0 and ts < CHUNK and ts >= CONV_WIDTH - 1
    assert nb % SAMPLE_GROUP == 0 and SAMPLE_GROUP * ts <= LANES
    assert dl % LRU_PAIR == 0 and ds % SGU_HEADS == 0

    tm = _pick_tile(math.gcd(mp, ms), (512, 256, 128, 64))
    tt = _pick_tile(math.gcd(seq, ms), (256, 128))
    tq = _pick_tile(math.gcd(seq, ms), (512, 256, 128, 64))
    ngroups = nb // SAMPLE_GROUP

    x_p, x_s = x_prompt.reshape(mp, d), jnp.transpose(x_sample, (1, 0, 2)).reshape(ms, d)

    ffn_w = [ffn1_w_in[0:1].astype(BF16), ffn1_w_down[0:1].astype(BF16), None, None]
    w_in_b, w_kv_b = w_in.astype(BF16), xa_w_kv.astype(BF16)
    k_tiles, v_tiles = _cache_tiles(cache_mem_k), _cache_tiles(cache_mem_v)

    kt_p, v_p, ktile_p, vtile_p = _mem_kv(mem_prompt.reshape(nbatch * mem, d), xa_norm[:, 2:3], w_kv_b, nbatch)

    outs = {k: [] for k in ("cvp", "hp", "cvs", "hs", "vs")}
    for l in range(depth):
        ffn1_wi, ffn1_wd = ffn_w[:2]
        if l == 0:
            x = _ffn(x_p, ffn1_norm[l:l + 1], ffn1_wi, ffn1_wd, 0, tm, xs=x_s)
        else:
            x = _ffn(x, ffn1_norm[l:l + 1], ffn1_wi, ffn1_wd, 0, tm)

        wg, bg = _gate_weights(lru_w_a[l], lru_b_a[l], lru_w_i[l], lru_b_i[l])
        cw, cb, lam = conv_w[l], conv_b[l][None, :], lru_lambda[l][None, :]
        sb_rows = jnp.repeat(sgu_b[l].T, ds // SGU_HEADS, axis=1)
        tri = jnp.tril(jnp.ones((ts, ts), F32))
        swx = jnp.repeat(jnp.transpose(sgu_w[l][:, :ts, :ts] * tri, (1, 2, 0)).reshape(ts * ts, SGU_HEADS),
                         ds // SGU_HEADS, axis=1)
        z_s = _norm_matmul(x, mix_norm[l, 0:1], w_in_b, l, tm, _pick_tile(w_in.shape[2], (2048, 1024)), F32,
                           "mix_in_sample", row_block0=mp // tm, rows=ms)
        y_s, conv_s, h_s, v_s = _mixer_sample(
            z_s, jnp.transpose(state_conv[l], (1, 0, 2)), state_lru_h[l], cw, cb, wg, bg, lam, sgu_ln[l],
            swx, sb_rows[:ts], nb, ts)
        y, xlast, hlast, *cast_w = _mix_prompt(
            x, mix_norm[l, 0:1], w_in_b, l, y_s, nbatch, seq, cw, cb, wg, bg, lam, sgu_ln[l], sgu_w[l], sb_rows, tt,
            cast=[(ffn2_w_in, 0), (ffn2_w_down, 0)] if l == 0 else ())
        if l == 0:
            ffn_w[2:] = [a[None] for a in cast_w]
        x = _matmul_norm_res(y, w_out, l, x, mix_norm[l, 1:2], tm, "mix_out")
        outs["cvp"].append(xlast[:, SUBLANES - (CONV_WIDTH - 1):, :])
        outs["hp"].append(hlast[:, 0, :])
        outs["cvs"].append(jnp.transpose(conv_s, (1, 0, 2)))
        outs["hs"].append(h_s)
        outs["vs"].append(jnp.transpose(v_s, (1, 0, 2)))

        q = _norm_matmul_f32w(x, xa_norm[l, 0:1], xa_w_q, l, tm, BF16, "xa_q")
        qt = q[mp:].reshape(ts, ngroups, SAMPLE_GROUP, XA_HEADS, hd)
        qt = jnp.transpose(qt, (1, 3, 4, 2, 0)).reshape(ngroups, XA_HEADS, hd, SAMPLE_GROUP * ts)
        o_s = _attn_sample(qt, k_tiles, v_tiles, l, ts).astype(BF16)
        o_s = jnp.transpose(o_s.reshape(ngroups, SAMPLE_GROUP, ts, d), (2, 0, 1, 3)).reshape(ms, d)
        o = _attn_prompt(q, kt_p, v_p, l, o_s, nbatch, seq, tq)
        x = _matmul_norm_res(o, xa_w_o, l, x, xa_norm[l, 1:2], tm, "xa_out")

        ffn2_wi, ffn2_wd = ffn_w[2:]
        if l < depth - 1:
            nxt = [(a, l + 1) for a in (ffn1_w_in, ffn1_w_down, ffn2_w_in, ffn2_w_down)]
            x, *ffn_w = _ffn(x, ffn2_norm[l:l + 1], ffn2_wi, ffn2_wd, 0, tm, cast=nxt)
            ffn_w = [a[None] for a in ffn_w]

    last_norm = ffn2_norm[depth - 1:depth]
    y_prompt = _ffn(x, last_norm, ffn2_wi, ffn2_wd, 0, tm, rows=mp).reshape(nbatch, seq, d)
    y_sample = _ffn(x, last_norm, ffn2_wi, ffn2_wd, 0, tm, row_block0=mp // tm, rows=ms)
    y_sample = jnp.transpose(y_sample.reshape(ts, nb, d), (1, 0, 2))
    return (y_prompt, y_sample, _tiles_to_heads(ktile_p, mem, XA_HEADS), _tiles_to_heads(vtile_p, mem, XA_HEADS),
            jnp.stack(outs["cvp"]),
            jnp.stack(outs["hp"]), jnp.stack(outs["cvs"]), jnp.stack(outs["hs"]), jnp.stack(outs["vs"]))
```
